```python
import jax, jax.numpy as jnp
from jax import lax
import numpy as np

D_MODEL = 2048
BATCH = 16
SEQ = 256
DEPTH = 4
DEC_BATCH = 2
DEC_SEQ = 1024
PAST_LEN = 256

F32 = jnp.float32
GRID_W = 64
BLOCK_Q = 128
CHUNK = 128
EPS = 1e-5
RW_LN_EPS = 64e-5
RET_HEADS = 4
RET_DK = 128
RET_DV = 128
RET_W = RET_HEADS * RET_DV
ATT_HEADS = 8
ATT_KV_HEADS = 2
ATT_HD = 64
ROPE_THETA = 10000.0
HY_C = 512
HY_ORDER = 2
HY_BANDS = 8
HY_FEAT = 1 + 2 * HY_BANDS
HY_HID = 64
HY_CONV_W = 3
RW_HEADS = 8
RW_N = 64
RW_C = RW_HEADS * RW_N
RW_LORA_W = 64
RW_LORA_A = 64
RW_LORA_G = 128
RW_SIZES = (RW_C, RW_C, RW_C, RW_LORA_W, RW_LORA_A, RW_LORA_G)
RW_TOTAL = 3 * RW_C + RW_LORA_W + RW_LORA_A + RW_LORA_G
N_BRANCH = 4
BRANCH_W = 512
IN_SIZES = (RET_HEADS * RET_DK, RET_HEADS * RET_DK, RET_W, RET_W,
            ATT_HEADS * ATT_HD, ATT_KV_HEADS * ATT_HD, ATT_KV_HEADS * ATT_HD,
            3 * HY_C, RW_TOTAL, N_BRANCH * D_MODEL)
IN_TOTAL = 2 * RET_HEADS * RET_DK + 2 * RET_W + (ATT_HEADS + 2 * ATT_KV_HEADS) * ATT_HD + 3 * HY_C + RW_TOTAL + N_BRANCH * D_MODEL
MOE_GROUPS = 4
MOE_PER_GROUP = 8
MOE_EXPERTS = MOE_GROUPS * MOE_PER_GROUP
MOE_HID = 256
MOE_TOPK = 2
N_MOD = 6
ALPHA = (2.0 * DEPTH) ** 0.25
BETA = (8.0 * DEPTH) ** -0.25

kernel_name = 'hybrid_diffusion_prefix_trunk_step'


def layer_norm(x, g=None, b=None, eps=EPS):
    xf = x.astype(F32)
    mu = jnp.mean(xf, -1, keepdims=True)
    var = jnp.mean(jnp.square(xf - mu), -1, keepdims=True)
    y = (xf - mu) * lax.rsqrt(var + eps)
    if g is not None:
        y = y * g.astype(F32) + b.astype(F32)
    return y


def rms_norm(x, g):
    xf = x.astype(F32)
    return xf * lax.rsqrt(jnp.mean(jnp.square(xf), -1, keepdims=True) + EPS) * g.astype(F32)


def split_cols(a, sizes):
    idx, acc = [], 0
    for s in sizes[:-1]:
        acc += s
        idx.append(acc)
    return jnp.split(a, idx, axis=-1)


def grid_positions(L):
    rows = L // GRID_W
    row = jnp.broadcast_to(jnp.arange(rows, dtype=F32)[:, None], (rows, GRID_W)).reshape(-1)
    col = jnp.broadcast_to(jnp.arange(GRID_W, dtype=F32)[None, :], (rows, GRID_W)).reshape(-1)
    return row, col


def rotate(x, pos):
    nf = x.shape[-1] // 2
    inv = ROPE_THETA ** (-jnp.arange(nf, dtype=F32) / nf)
    ang = pos[:, None] * inv[None, :]
    cos = jnp.cos(ang)[None, :, None, :]
    sin = jnp.sin(ang)[None, :, None, :]
    x1, x2 = x[..., :nf], x[..., nf:]
    return jnp.concatenate([x1 * cos - x2 * sin, x1 * sin + x2 * cos], -1)


def axial_rope(x):
    row, col = grid_positions(x.shape[1])
    half = x.shape[-1] // 2
    return jnp.concatenate([rotate(x[..., :half], row), rotate(x[..., half:], col)], -1)


def block_attention(q, k, v):
    B, Lq, H, d = q.shape
    kvh = k.shape[2]
    grp = H // kvh
    nb = Lq // BLOCK_Q
    qb = q.reshape(B, nb, BLOCK_Q, kvh, grp, d).transpose(1, 0, 2, 3, 4, 5)
    scale = d ** -0.5

    def one_block(qblk):
        s = jnp.einsum('bqkgd,bskd->bkgqs', qblk, k) * scale
        p = jax.nn.softmax(s, axis=-1)
        return jnp.einsum('bkgqs,bskd->bqkgd', p, v)

    o = lax.map(one_block, qb)
    return o.transpose(1, 0, 2, 3, 4, 5).reshape(B, Lq, H * d)


def retention_scan(q, k, v, log_gamma, s0):
    B, L, H, _ = q.shape
    dv = v.shape[-1]
    nc = L // CHUNK
    idx = jnp.arange(CHUNK, dtype=F32)
    rel = idx[:, None] - idx[None, :]
    lg = log_gamma.astype(F32)
    dmask = jnp.where(rel[None] >= 0, jnp.exp(lg[:, None, None] * jnp.maximum(rel, 0.0)[None]), 0.0)
    q_dec = jnp.exp(lg[None, :] * (idx[:, None] + 1.0))
    k_dec = jnp.exp(lg[:, None] * (CHUNK - 1.0 - idx[None, :]))
    c_dec = jnp.exp(lg * CHUNK)

    def chunks(a):
        return a.reshape(B, nc, CHUNK, H, a.shape[-1]).transpose(1, 0, 2, 3, 4)

    def step(s, xs):
        qc, kc, vc = xs
        att = jnp.einsum('bihd,bjhd->bhij', qc, kc) * dmask[None]
        inner = jnp.einsum('bhij,bjhv->bihv', att, vc)
        cross = jnp.einsum('bihd,bhdv->bihv', qc, s) * q_dec[None, :, :, None]
        s_new = s * c_dec[None, :, None, None] + jnp.einsum('bjhd,bjhv,hj->bhdv', kc, vc, k_dec)
        return s_new, inner + cross

    s_fin, o = lax.scan(step, s0.astype(F32), (chunks(q), chunks(k), chunks(v)))
    return o.transpose(1, 0, 2, 3, 4).reshape(B, L, H, dv), s_fin


def retention_branch(rq, rk, rv, rg, decay_exp, s0):
    B, L, _ = rq.shape
    q = rq.astype(F32).reshape(B, L, RET_HEADS, RET_DK)
    k = rk.astype(F32).reshape(B, L, RET_HEADS, RET_DK) * RET_DK ** -0.5
    v = rv.astype(F32).reshape(B, L, RET_HEADS, RET_DV)
    log_gamma = jnp.log1p(-jnp.exp2(-decay_exp.astype(F32)))
    o_f, s_f = retention_scan(q, k, v, log_gamma[0], s0[:, 0])
    o_b, s_b = retention_scan(q[:, ::-1], k[:, ::-1], v[:, ::-1], log_gamma[1], s0[:, 1])
    o = layer_norm(o_f + o_b[:, ::-1]).reshape(B, L, RET_W)
    return o * jax.nn.silu(rg.astype(F32)), jnp.stack([s_f, s_b], 1)


def centred_dwconv(u, w):
    K, C = w.shape
    return lax.conv_general_dilated(u, w[:, None, :].astype(u.dtype), window_strides=(1,),
                                    padding=[(K // 2, K // 2)],
                                    dimension_numbers=('NWC', 'WIO', 'NWC'),
                                    feature_group_count=C)


def hyena_filters(L, w1, b1, w2, b2, freq, w3, log_rate):
    t = jnp.arange(L, dtype=F32) / L
    bands = jnp.arange(1, HY_BANDS + 1, dtype=F32)
    ang = 2.0 * jnp.pi * t[:, None] * bands[None, :]
    z = jnp.concatenate([t[:, None], jnp.sin(ang), jnp.cos(ang)], -1)
    h = jnp.sin(freq[0] * (z @ w1 + b1))
    h = jnp.sin(freq[1] * (h @ w2 + b2))
    h = (h @ w3).reshape(L, HY_ORDER, HY_C)
    center = L // 2
    dist = jnp.abs(jnp.arange(L, dtype=F32) - center) / (0.5 * L)
    window = jnp.exp(-jnp.exp(log_rate.astype(F32))[None] * dist[:, None, None])
    return h * window


def fft_centred_conv(u, h):
    L = u.shape[1]
    U = jnp.fft.rfft(u, n=2 * L, axis=1)
    Hf = jnp.fft.rfft(h, n=2 * L, axis=0)
    y = jnp.fft.irfft(U * Hf[None], n=2 * L, axis=1)
    return y[:, L // 2: L // 2 + L]


def hyena_branch(u, conv_w, w1, b1, w2, b2, freq, w3, log_rate, bias):
    L = u.shape[1]
    u = centred_dwconv(u, conv_w).astype(F32)
    v, x1, x2 = jnp.split(u, 3, axis=-1)
    h = hyena_filters(L, w1, b1, w2, b2, freq, w3, log_rate)
    bias = bias.astype(F32)
    z = fft_centred_conv(v, h[:, 0]) + bias[0] * v
    z = x1 * z
    z = fft_centred_conv(z, h[:, 1]) + bias[1] * z
    return x2 * z


def token_shift_centred(p, mu):
    prev = jnp.pad(p, ((0, 0), (1, 0), (0, 0)))[:, :-1]
    nxt = jnp.pad(p, ((0, 0), (0, 1), (0, 0)))[:, 1:]
    return p + mu * (0.5 * (prev + nxt) - p)


def rwkv_scan(r, w, k, v, a, b, s0):
    def step(s, xs):
        rt, wt, kt, vt, at, bt = xs
        sa = jnp.einsum('bhvk,bhk->bhv', s, at)
        s = s * wt[:, :, None, :] + sa[..., None] * bt[:, :, None, :] + vt[..., None] * kt[:, :, None, :]
        return s, jnp.einsum('bhvk,bhk->bhv', s, rt)

    xs = tuple(jnp.moveaxis(t, 1, 0) for t in (r, w, k, v, a, b))
    s_fin, y = lax.scan(step, s0.astype(F32), xs)
    return jnp.moveaxis(y, 0, 1), s_fin


def rwkv_branch(p, mu, w0, w_up, a0, a_up, g_up, k_k, k_a, r_k, s0):
    B, L, _ = p.shape
    p = token_shift_centred(p.astype(F32), mu.astype(F32))
    r, k, v, w_low, a_low, g_low = split_cols(p, RW_SIZES)
    iclr = jax.nn.sigmoid(a0 + a_low @ a_up)
    g = jax.nn.sigmoid(g_low) @ g_up

    def heads(t):
        return t.reshape(B, L, RW_HEADS, RW_N)

    kk = heads(k * k_k)
    kk = kk * lax.rsqrt(jnp.sum(jnp.square(kk), -1, keepdims=True) + 1e-12)
    k = heads(k * (1.0 + (iclr - 1.0) * k_a))
    r, v, a = heads(r), heads(v), heads(iclr)
    a_vec, b_vec = -kk, kk * a
    ys, states = [], []
    for d in range(2):
        log_w = -jax.nn.softplus(-(w0[d] + jnp.tanh(w_low) @ w_up[d])) - 0.5
        decay = heads(jnp.exp(-jnp.exp(log_w)))
        seq = (r, decay, k, v, a_vec, b_vec)
        if d == 1:
            seq = tuple(t[:, ::-1] for t in seq)
        y, s = rwkv_scan(*seq, s0[:, d])
        ys.append(y if d == 0 else y[:, ::-1])
        states.append(s)
    bonus = jnp.sum(r * k * r_k.astype(F32), -1, keepdims=True) * v
    y = (layer_norm(ys[0] + ys[1], eps=RW_LN_EPS) + bonus).reshape(B, L, RW_C)
    return y * g, jnp.stack(states, 1)


def hier_moe(h, rg_w, rg_b, re_w, re_b, w_gate, w_up, w_down):
    B, L, D = h.shape
    x = h.reshape(-1, D)
    n = x.shape[0]
    glog = (x @ rg_w + rg_b).astype(F32)
    gprob = jax.nn.softmax(glog, -1)
    _, gsel = lax.top_k(glog, 1)
    gval = jnp.take_along_axis(gprob, gsel, 1)
    elog = (x @ re_w + re_b).astype(F32).reshape(n, MOE_GROUPS, MOE_PER_GROUP)
    gidx = jnp.broadcast_to(gsel[:, :, None], (n, 1, MOE_PER_GROUP))
    elog_g = jnp.take_along_axis(elog, gidx, 1)[:, 0]
    eprob = jax.nn.softmax(elog_g, -1)
    top_v, top_i = lax.top_k(eprob, MOE_TOPK)
    top_v = top_v / jnp.sum(top_v, -1, keepdims=True)
    expert_idx = gsel * MOE_PER_GROUP + top_i
    weights = gval * top_v
    combine = jnp.sum(jax.nn.one_hot(expert_idx, MOE_EXPERTS, dtype=F32) * weights[..., None], 1)
    hg = jnp.einsum('nd,edh->neh', x, w_gate).astype(F32)
    hu = jnp.einsum('nd,edh->neh', x, w_up).astype(F32)
    act = (jax.nn.silu(hg) * hu * combine[..., None]).astype(h.dtype)
    y = jnp.einsum('neh,ehd->nd', act, w_down)
    return y.reshape(B, L, D)


def mixer_sublayer(h, lp, ctx):
    B, L, _ = h.shape
    rq, rk, rv, rg, aq, ak, av, hy_u, rw_p, gate_logits = split_cols(h @ lp['w_in'], IN_SIZES)
    if ctx is None:
        ret_s0 = jnp.zeros((B, 2, RET_HEADS, RET_DK, RET_DV), F32)
        rw_s0 = jnp.zeros((B, 2, RW_HEADS, RW_N, RW_N), F32)
    else:
        ret_s0, rw_s0 = ctx['ret'], ctx['rw']
    o_a, ret_s = retention_branch(rq, rk, rv, rg, lp['ret_decay_exp'], ret_s0)
    q = rms_norm(aq.reshape(B, L, ATT_HEADS, ATT_HD), lp['attn_q_norm'])
    k = rms_norm(ak.reshape(B, L, ATT_KV_HEADS, ATT_HD), lp['attn_k_norm'])
    v = av.reshape(B, L, ATT_KV_HEADS, ATT_HD).astype(F32)
    if ctx is None:
        o_b = block_attention(q, k, v)
    else:
        k_all = jnp.concatenate([ctx['k'].astype(F32), axial_rope(k)], 1)
        v_all = jnp.concatenate([ctx['v'].astype(F32), v], 1)
        o_b = block_attention(axial_rope(q), k_all, v_all)
    o_c = hyena_branch(hy_u, lp['hy_conv'], lp['hy_w1'], lp['hy_b1'], lp['hy_w2'], lp['hy_b2'],
                       lp['hy_freq'], lp['hy_w3'], lp['hy_log_rate'], lp['hy_bias'])
    o_d, rw_s = rwkv_branch(rw_p, lp['rw_mu'], lp['rw_w0'], lp['rw_w_up'], lp['rw_a0'], lp['rw_a_up'],
                            lp['rw_g_up'], lp['rw_k_k'], lp['rw_k_a'], lp['rw_r_k'], rw_s0)
    branches = jnp.stack([o_a, o_b, o_c, o_d], 2).astype(h.dtype)
    proj = jnp.einsum('blnc,ncd->blnd', branches, lp['w_branch'])
    gates = jax.nn.sigmoid(gate_logits.astype(F32)).reshape(B, L, N_BRANCH, D_MODEL)
    out = jnp.sum(gates * proj, 2).astype(h.dtype) @ lp['w_out']
    ctx_out = (k.astype(h.dtype), v.astype(h.dtype), ret_s.astype(h.dtype), rw_s.astype(h.dtype))
    return out, ctx_out


def modulation(cvec, mod_w, mod_b):
    return (jax.nn.silu(cvec.astype(F32)) @ mod_w + mod_b).astype(F32).reshape(cvec.shape[0], N_MOD, D_MODEL)


def trunk_layer(x, mod, lp, ctx):
    sh1, sc1, g1, sh2, sc2, g2 = (mod[:, j][:, None, :] for j in range(N_MOD))
    h = (layer_norm(x) * (1.0 + sc1) + sh1).astype(x.dtype)
    mix, ctx_out = mixer_sublayer(h, lp, ctx)
    x = layer_norm(ALPHA * x.astype(F32) + g1 * mix.astype(F32), lp['ln_g'][0], lp['ln_b'][0]).astype(x.dtype)
    h = (layer_norm(x) * (1.0 + sc2) + sh2).astype(x.dtype)
    ffn = hier_moe(h, lp['moe_rg_w'], lp['moe_rg_b'], lp['moe_re_w'], lp['moe_re_b'],
                   lp['moe_w_gate'], lp['moe_w_up'], lp['moe_w_down'])
    x = layer_norm(ALPHA * x.astype(F32) + g2 * ffn.astype(F32), lp['ln_g'][1], lp['ln_b'][1]).astype(x.dtype)
    return x, ctx_out


def setup_inputs(seed: int = 0) -> dict:
    key = jax.random.key(seed)
    ks = iter(jax.random.split(key, 64))

    def nrm(shape, scale=1.0):
        return jax.random.normal(next(ks), shape, F32) * scale

    def uni(shape, lo, hi):
        return jax.random.uniform(next(ks), shape, F32, lo, hi)

    d = D_MODEL
    return {
        'x_prompt': nrm((BATCH, SEQ, d)),
        'x_sample': nrm((DEC_BATCH, DEC_SEQ, d)),
        'c': nrm((DEC_BATCH, d)),
        'cache_attn_k': nrm((DEC_BATCH, DEPTH, PAST_LEN, ATT_KV_HEADS, ATT_HD)),
        'cache_attn_v': nrm((DEC_BATCH, DEPTH, PAST_LEN, ATT_KV_HEADS, ATT_HD)),
        'state_ret': nrm((DEC_BATCH, DEPTH, 2, RET_HEADS, RET_DK, RET_DV)),
        'state_rwkv': nrm((DEC_BATCH, DEPTH, 2, RW_HEADS, RW_N, RW_N)),
        'c_ctx': nrm((d,)),
        'mod_w': nrm((DEPTH, d, N_MOD * d), 0.5 * d ** -0.5),
        'mod_b': nrm((DEPTH, N_MOD * d), 0.01),
        'w_in': nrm((DEPTH, d, IN_TOTAL), d ** -0.5),
        'ret_decay_exp': 5.0 + jnp.arange(RET_HEADS, dtype=F32) + uni((DEPTH, 2, RET_HEADS), 0.0, 0.5),
        'attn_q_norm': 1.0 + nrm((DEPTH, ATT_HD), 0.1),
        'attn_k_norm': 1.0 + nrm((DEPTH, ATT_HD), 0.1),
        'hy_conv': nrm((DEPTH, HY_CONV_W, 3 * HY_C), HY_CONV_W ** -0.5),
        'hy_w1': nrm((DEPTH, HY_FEAT, HY_HID), HY_FEAT ** -0.5),
        'hy_b1': nrm((DEPTH, HY_HID), 0.1),
        'hy_w2': nrm((DEPTH, HY_HID, HY_HID), HY_HID ** -0.5),
        'hy_b2': nrm((DEPTH, HY_HID), 0.1),
        'hy_freq': 1.0 + nrm((DEPTH, 2, HY_HID), 0.1),
        'hy_w3': nrm((DEPTH, HY_HID, HY_ORDER * HY_C), 0.1 * HY_HID ** -0.5),
        'hy_log_rate': jnp.log(uni((DEPTH, HY_ORDER, HY_C), 3.0, 15.0)),
        'hy_bias': nrm((DEPTH, HY_ORDER, HY_C)),
        'rw_mu': uni((DEPTH, RW_TOTAL), 0.0, 1.0),
        'rw_w0': jnp.linspace(-6.0, 1.0, RW_C, dtype=F32) + nrm((DEPTH, 2, RW_C), 0.1),
        'rw_w_up': nrm((DEPTH, 2, RW_LORA_W, RW_C), 0.5 * RW_LORA_W ** -0.5),
        'rw_a0': nrm((DEPTH, RW_C), 0.1),
        'rw_a_up': nrm((DEPTH, RW_LORA_A, RW_C), RW_LORA_A ** -0.5),
        'rw_g_up': nrm((DEPTH, RW_LORA_G, RW_C), RW_LORA_G ** -0.5),
        'rw_k_k': 0.85 + nrm((DEPTH, RW_C), 0.05),
        'rw_k_a': 1.0 + nrm((DEPTH, RW_C), 0.05),
        'rw_r_k': nrm((DEPTH, RW_HEADS, RW_N), 0.1),
        'w_branch': nrm((DEPTH, N_BRANCH, BRANCH_W, d), BETA * BRANCH_W ** -0.5),
        'w_out': nrm((DEPTH, d, d), BETA * d ** -0.5),
        'ln_g': 1.0 + nrm((DEPTH, 2, d), 0.05),
        'ln_b': nrm((DEPTH, 2, d), 0.05),
        'moe_rg_w': nrm((DEPTH, d, MOE_GROUPS), d ** -0.5),
        'moe_rg_b': nrm((DEPTH, MOE_GROUPS), 0.01),
        'moe_re_w': nrm((DEPTH, d, MOE_EXPERTS), d ** -0.5),
        'moe_re_b': nrm((DEPTH, MOE_EXPERTS), 0.01),
        'moe_w_gate': nrm((DEPTH, MOE_EXPERTS, d, MOE_HID), d ** -0.5),
        'moe_w_up': nrm((DEPTH, MOE_EXPERTS, d, MOE_HID), d ** -0.5),
        'moe_w_down': nrm((DEPTH, MOE_EXPERTS, MOE_HID, d), BETA * MOE_HID ** -0.5),
    }


def reference(x_prompt, x_sample, c, cache_attn_k, cache_attn_v, state_ret, state_rwkv, c_ctx,
              mod_w, mod_b, w_in, ret_decay_exp, attn_q_norm, attn_k_norm,
              hy_conv, hy_w1, hy_b1, hy_w2, hy_b2, hy_freq, hy_w3, hy_log_rate, hy_bias,
              rw_mu, rw_w0, rw_w_up, rw_a0, rw_a_up, rw_g_up, rw_k_k, rw_k_a, rw_r_k,
              w_branch, w_out, ln_g, ln_b,
              moe_rg_w, moe_rg_b, moe_re_w, moe_re_b, moe_w_gate, moe_w_up, moe_w_down):
    stacked = {
        'w_in': w_in, 'ret_decay_exp': ret_decay_exp, 'attn_q_norm': attn_q_norm, 'attn_k_norm': attn_k_norm,
        'hy_conv': hy_conv, 'hy_w1': hy_w1, 'hy_b1': hy_b1, 'hy_w2': hy_w2, 'hy_b2': hy_b2,
        'hy_freq': hy_freq, 'hy_w3': hy_w3, 'hy_log_rate': hy_log_rate, 'hy_bias': hy_bias,
        'rw_mu': rw_mu, 'rw_w0': rw_w0, 'rw_w_up': rw_w_up, 'rw_a0': rw_a0, 'rw_a_up': rw_a_up,
        'rw_g_up': rw_g_up, 'rw_k_k': rw_k_k, 'rw_k_a': rw_k_a, 'rw_r_k': rw_r_k,
        'w_branch': w_branch, 'w_out': w_out, 'ln_g': ln_g, 'ln_b': ln_b,
        'moe_rg_w': moe_rg_w, 'moe_rg_b': moe_rg_b, 'moe_re_w': moe_re_w, 'moe_re_b': moe_re_b,
        'moe_w_gate': moe_w_gate, 'moe_w_up': moe_w_up, 'moe_w_down': moe_w_down,
    }

    y = x_prompt
    ks_out, vs_out, ret_out, rw_out = [], [], [], []
    for i in range(DEPTH):
        lp = {name: arr[i] for name, arr in stacked.items()}
        mod = modulation(c_ctx[None, :], mod_w[i], mod_b[i])
        y, (k_i, v_i, ret_i, rw_i) = trunk_layer(y, mod, lp, None)
        ks_out.append(k_i)
        vs_out.append(v_i)
        ret_out.append(ret_i)
        rw_out.append(rw_i)
    y_prompt = y
    new_attn_k = jnp.stack(ks_out, 1)
    new_attn_v = jnp.stack(vs_out, 1)
    new_state_ret = jnp.stack(ret_out, 1)
    new_state_rwkv = jnp.stack(rw_out, 1)

    ys = x_sample
    for i in range(DEPTH):
        lp = {name: arr[i] for name, arr in stacked.items()}
        mod = modulation(c, mod_w[i], mod_b[i])
        ctx = {'k': cache_attn_k[:, i], 'v': cache_attn_v[:, i], 'ret': state_ret[:, i], 'rw': state_rwkv[:, i]}
        ys, _ = trunk_layer(ys, mod, lp, ctx)
    y_sample = ys

    return (y_prompt, y_sample, new_attn_k, new_attn_v, new_state_ret, new_state_rwkv)
```

```python
import functools
import math

import jax
import jax.numpy as jnp
from jax import lax
from jax.experimental import pallas as pl
from jax.experimental.pallas import tpu as pltpu

F32 = jnp.float32
BF16 = jnp.bfloat16

D_MODEL = 2048
DEPTH = 4
GRID_W = 64
RET_CHUNK = 128
EPS = 1e-5
RW_LN_EPS = 64e-5
RET_HEADS = 4
RET_DK = 128
RET_DV = 128
ATT_HEADS = 8
ATT_KV_HEADS = 2
ATT_HD = 64
ROPE_THETA = 10000.0
HY_C = 512
HY_BANDS = 8
HY_FEAT = 1 + 2 * HY_BANDS
HY_HID = 64
RW_HEADS = 8
RW_N = 64
RW_C = RW_HEADS * RW_N
RW_CHUNK = 64
N_BRANCH = 4
BRANCH_W = 512
MOE_GROUPS = 4
MOE_PER_GROUP = 8
MOE_EXPERTS = MOE_GROUPS * MOE_PER_GROUP
MOE_HID = 256
N_MOD = 6
ALPHA = (2.0 * DEPTH) ** 0.25

MIX_W = 6144
COL_TILE = 256
LANES = 128
VMEM_LIMIT = 56 * 1024 * 1024

T_RQ, T_RK, T_RV, T_RG = 0, 2, 4, 6
T_AQ, T_AKV = 8, 10
T_HY = 11
T_RW = 17


def _cparams(sem):
    return pltpu.CompilerParams(dimension_semantics=sem, vmem_limit_bytes=VMEM_LIMIT)


def _dot(a, b):
    return jnp.dot(a.astype(BF16), b.astype(BF16), preferred_element_type=F32)


def _dot_nt(a, b):
    return lax.dot_general(a.astype(BF16), b.astype(BF16), (((1,), (1,)), ((), ())), preferred_element_type=F32)


def _dot_tn(a, b):
    return lax.dot_general(a.astype(BF16), b.astype(BF16), (((0,), (0,)), ((), ())), preferred_element_type=F32)


def _split(x):
    hi = x.astype(BF16)
    lo = (x - hi.astype(F32)).astype(BF16)
    return hi, lo


def _dot_x2(x, m):
    hi, lo = _split(x)
    return jnp.dot(hi, m, preferred_element_type=F32) + jnp.dot(lo, m, preferred_element_type=F32)


def _dot_m2x(m, x):
    hi, lo = _split(x)
    return jnp.dot(m, hi, preferred_element_type=F32) + jnp.dot(m, lo, preferred_element_type=F32)


def _dot3(x, y):
    xh, xl = _split(x)
    yh, yl = _split(y)
    return (jnp.dot(xh, yh, preferred_element_type=F32) + jnp.dot(xl, yh, preferred_element_type=F32)
            + jnp.dot(xh, yl, preferred_element_type=F32))


def _dot3_pre(mh, ml, x):
    xh, xl = _split(x)
    return (jnp.dot(mh, xh, preferred_element_type=F32) + jnp.dot(mh, xl, preferred_element_type=F32)
            + jnp.dot(ml, xh, preferred_element_type=F32))


def _ln(x, eps=EPS):
    mu = jnp.mean(x, axis=-1, keepdims=True)
    xc = x - mu
    var = jnp.mean(xc * xc, axis=-1, keepdims=True)
    return xc * lax.rsqrt(var + eps)


def _silu(x):
    return x * jax.nn.sigmoid(x)


MOD_TN = 1024


def _mod_kernel(c_ref, w_ref, b_ref, o_ref):
    s = _silu(c_ref[...])
    o_ref[...] = _dot(s, w_ref[...]) + b_ref[...]


def _modulation(cvec, mod_w, mod_b):
    per = D_MODEL // MOD_TN
    return pl.pallas_call(
        _mod_kernel,
        grid=(DEPTH, N_MOD * per),
        in_specs=[
            pl.BlockSpec((8, D_MODEL), lambda l, n: (0, 0)),
            pl.BlockSpec((None, D_MODEL, MOD_TN), lambda l, n: (l, 0, n)),
            pl.BlockSpec((None, 1, MOD_TN), lambda l, n: (l, 0, n)),
        ],
        out_specs=pl.BlockSpec((None, None, 8, MOD_TN), lambda l, n: (l, n // per, 0, n % per)),
        out_shape=jax.ShapeDtypeStruct((DEPTH, N_MOD, 8, D_MODEL), F32),
        compiler_params=_cparams(("arbitrary", "arbitrary")),
        name="modulation",
    )(cvec, mod_w, mod_b.reshape(DEPTH, 1, N_MOD * D_MODEL))


INP_TM = 1024
INP_TN = 512


def _inproj_kernel(x_ref, mod_ref, w_ref, p_ref, h_ref, *, req_base, rows_per_req):
    i = pl.program_id(0)
    j = pl.program_id(1)

    @pl.when(j == 0)
    def _():
        m = req_base + (i * INP_TM) // rows_per_req
        sh = mod_ref[0, pl.ds(m, 1), :]
        sc = mod_ref[1, pl.ds(m, 1), :]
        h_ref[...] = (_ln(x_ref[...]) * (1.0 + sc) + sh).astype(BF16)

    p_ref[...] = jnp.dot(h_ref[...], w_ref[...].astype(BF16), preferred_element_type=F32)


def _inproj(x2d, mod_l, w_in, layer, req_base, rows_per_req):
    n = x2d.shape[0]
    kern = functools.partial(_inproj_kernel, req_base=req_base, rows_per_req=rows_per_req)
    return pl.pallas_call(
        kern,
        grid=(n // INP_TM, MIX_W // INP_TN),
        in_specs=[
            pl.BlockSpec((INP_TM, D_MODEL), lambda i, j: (i, 0)),
            pl.BlockSpec((N_MOD, 8, D_MODEL), lambda i, j: (0, 0, 0)),
            pl.BlockSpec((None, D_MODEL, INP_TN), lambda i, j: (layer, 0, j)),
        ],
        out_specs=[
            pl.BlockSpec((INP_TM, INP_TN), lambda i, j: (i, j)),
            pl.BlockSpec((INP_TM, D_MODEL), lambda i, j: (i, 0)),
        ],
        out_shape=[jax.ShapeDtypeStruct((n, MIX_W), F32), jax.ShapeDtypeStruct((n, D_MODEL), BF16)],
        compiler_params=_cparams(("arbitrary", "arbitrary")),
        name="in_projection",
    )(x2d, mod_l, w_in)


def _retention_kernel(*refs, seq, has_s0, emit_state):
    q_ref, k_ref, v_ref, g_ref, dec_ref = refs[:5]
    pos = 5
    s0_ref = None
    if has_s0:
        s0_ref = refs[pos]
        pos += 1
    o_ref = refs[pos]
    pos += 1
    st_ref = None
    if emit_state:
        st_ref = refs[pos]
        pos += 1
    s_scr, o_scr = refs[pos], refs[pos + 1]

    C = RET_CHUNK
    nc = seq // C
    lg_all = jnp.log1p(-jnp.exp2(-dec_ref[...]))
    ii = lax.broadcasted_iota(jnp.int32, (C, C), 0)
    jj = lax.broadcasted_iota(jnp.int32, (C, C), 1)
    rel = (ii - jj).astype(F32)
    icol = lax.broadcasted_iota(jnp.int32, (C, 1), 0).astype(F32)

    for d in range(2):
        for h in range(RET_HEADS):
            if has_s0:
                s_scr[d, h] = s0_ref[d, h]
            else:
                s_scr[d, h] = jnp.zeros((RET_DK, RET_DV), F32)

    for d in range(2):
        consts = []
        for h in range(RET_HEADS):
            lg = lg_all[d:d + 1, h * RET_DV:(h + 1) * RET_DV]
            lg1 = lg[:, :1]
            if d == 0:
                dmask = jnp.where(rel >= 0, jnp.exp(lg * rel), 0.0)
                q_dec = jnp.exp(lg1 * (icol + 1.0))
                k_dec = jnp.exp(lg1 * (C - 1.0 - icol))
            else:
                dmask = jnp.where(rel <= 0, jnp.exp(lg * (-rel)), 0.0)
                q_dec = jnp.exp(lg1 * (C - icol))
                k_dec = jnp.exp(lg1 * icol)
            consts.append((dmask, q_dec, k_dec, jnp.exp(lg * float(C))))

        def body(ci, carry, d=d, consts=consts):
            c = ci if d == 0 else nc - 1 - ci
            r0 = pl.multiple_of(c * C, C)
            for h in range(RET_HEADS):
                dmask, q_dec, k_dec, c_dec = consts[h]
                cs = slice(h * RET_DK, (h + 1) * RET_DK)
                qc = q_ref[pl.ds(r0, C), cs]
                kc = k_ref[pl.ds(r0, C), cs] * (RET_DK ** -0.5)
                vc = v_ref[pl.ds(r0, C), cs]
                s = s_scr[d, h]
                att = _dot_nt(qc, kc) * dmask
                out = _dot(att, vc) + _dot(qc, s) * q_dec
                s_scr[d, h] = s * c_dec + _dot_tn(kc * k_dec, vc)
                if d == 0:
                    o_scr[pl.ds(r0, C), cs] = out
                else:
                    o_scr[pl.ds(r0, C), cs] = o_scr[pl.ds(r0, C), cs] + out
            return carry

        lax.fori_loop(0, nc, body, 0)

    for h in range(RET_HEADS):
        cs = slice(h * RET_DV, (h + 1) * RET_DV)
        o_ref[:, cs] = (_ln(o_scr[:, cs]) * _silu(g_ref[:, cs])).astype(BF16)
    if emit_state:
        st_ref[...] = s_scr[...]


def _retention(p, dec_rep, s0, batch, seq, emit_state):
    n = batch * seq
    w = RET_HEADS * RET_DK
    has_s0 = s0 is not None
    kern = functools.partial(_retention_kernel, seq=seq, has_s0=has_s0, emit_state=emit_state)
    in_specs = [
        pl.BlockSpec((seq, w), lambda b: (b, 0)),
        pl.BlockSpec((seq, w), lambda b: (b, 1)),
        pl.BlockSpec((seq, w), lambda b: (b, 2)),
        pl.BlockSpec((seq, w), lambda b: (b, 3)),
        pl.BlockSpec((2, w), lambda b: (0, 0)),
    ]
    args = [p, p, p, p, dec_rep]
    if has_s0:
        in_specs.append(pl.BlockSpec((None, 2, RET_HEADS, RET_DK, RET_DV), lambda b: (b, 0, 0, 0, 0)))
        args.append(s0)
    out_specs = [pl.BlockSpec((seq, w), lambda b: (b, 0))]
    out_shape = [jax.ShapeDtypeStruct((n, w), BF16)]
    if emit_state:
        out_specs.append(pl.BlockSpec((None, 2, RET_HEADS, RET_DK, RET_DV), lambda b: (b, 0, 0, 0, 0)))
        out_shape.append(jax.ShapeDtypeStruct((batch, 2, RET_HEADS, RET_DK, RET_DV), F32))
    return pl.pallas_call(
        kern,
        grid=(batch,),
        in_specs=in_specs,
        out_specs=out_specs,
        out_shape=out_shape,
        scratch_shapes=[pltpu.VMEM((2, RET_HEADS, RET_DK, RET_DV), F32), pltpu.VMEM((seq, w), F32)],
        compiler_params=_cparams(("arbitrary",)),
        name="retention",
    )(*args)


ATT_QB = 256


def _head_rms(x, ones_bd, gain):
    ss = _dot_x2(x * x, ones_bd)
    return x * lax.rsqrt(ss * (1.0 / ATT_HD) + EPS) * gain


def _rope(x, cos, sin_signed):
    w = x.shape[-1]
    lane = lax.broadcasted_iota(jnp.int32, x.shape, 1)
    first = (lane & 16) == 0
    swapped = jnp.where(first, pltpu.roll(x, w - 16, axis=1), pltpu.roll(x, 16, axis=1))
    return x * cos + swapped * sin_signed


def _attention_kernel(*refs, seq, decode, past):
    q_ref, kv_ref, gq_ref, gk_ref, bdq_ref, bdk_ref = refs[:6]
    pos = 6
    if decode:
        ck_ref, cv_ref, cos_ref, sin_ref = refs[pos:pos + 4]
        pos += 4
    o_ref = refs[pos]
    pos += 1
    if not decode:
        ko_ref, vo_ref = refs[pos:pos + 2]
        pos += 2
    q_scr, k_scr, v_scr = refs[pos:pos + 3]

    kvw = ATT_KV_HEADS * ATT_HD
    q = _head_rms(q_ref[...], bdq_ref[...], gq_ref[...])
    kv = kv_ref[...]
    k = _head_rms(kv[:, :kvw], bdk_ref[...], gk_ref[...])
    v = kv[:, kvw:]
    if decode:
        cos = cos_ref[...]
        sin = sin_ref[...]
        q = _rope(q, cos, sin)
        k = _rope(k, cos[:, :kvw], sin[:, :kvw])
        k_scr[0:past, :] = ck_ref[...].astype(BF16)
        v_scr[0:past, :] = cv_ref[...].astype(BF16)
        k_scr[past:past + seq, :] = k.astype(BF16)
        v_scr[past:past + seq, :] = v.astype(BF16)
    else:
        ko_ref[...] = k
        vo_ref[...] = v
        k_scr[...] = k.astype(BF16)
        v_scr[...] = v.astype(BF16)
    q_scr[...] = (q * (ATT_HD ** -0.5)).astype(BF16)

    grp = ATT_HEADS // ATT_KV_HEADS

    def body(qi, carry):
        r0 = pl.multiple_of(qi * ATT_QB, ATT_QB)
        outs = []
        for h in range(ATT_HEADS):
            g = h // grp
            qh = q_scr[pl.ds(r0, ATT_QB), h * ATT_HD:(h + 1) * ATT_HD]
            kh = k_scr[:, g * ATT_HD:(g + 1) * ATT_HD]
            vh = v_scr[:, g * ATT_HD:(g + 1) * ATT_HD]
            s = lax.dot_general(qh, kh, (((1,), (1,)), ((), ())), preferred_element_type=F32)
            s = s - jnp.max(s, axis=-1, keepdims=True)
            e = jnp.exp(s)
            prob = e / jnp.sum(e, axis=-1, keepdims=True)
            outs.append(jnp.dot(prob.astype(BF16), vh, preferred_element_type=F32))
        o_ref[pl.ds(r0, ATT_QB), :] = jnp.concatenate(outs, axis=1).astype(BF16)
        return carry

    lax.fori_loop(0, seq // ATT_QB, body, 0)


def _attention(p, gq, gk, bdq, bdk, batch, seq, cache=None):
    n = batch * seq
    qw = ATT_HEADS * ATT_HD
    kvw = ATT_KV_HEADS * ATT_HD
    decode = cache is not None
    past = cache[0].shape[1] if decode else 0
    kern = functools.partial(_attention_kernel, seq=seq, decode=decode, past=past)
    in_specs = [
        pl.BlockSpec((seq, qw), lambda b: (b, T_AQ * COL_TILE // qw)),
        pl.BlockSpec((seq, 2 * kvw), lambda b: (b, T_AKV * COL_TILE // (2 * kvw))),
        pl.BlockSpec((1, qw), lambda b: (0, 0)),
        pl.BlockSpec((1, kvw), lambda b: (0, 0)),
        pl.BlockSpec((qw, qw), lambda b: (0, 0)),
        pl.BlockSpec((kvw, kvw), lambda b: (0, 0)),
    ]
    args = [p, p, gq, gk, bdq, bdk]
    out_specs = [pl.BlockSpec((seq, qw), lambda b: (b, 0))]
    out_shape = [jax.ShapeDtypeStruct((n, qw), BF16)]
    if decode:
        ck, cv, cos, sin = cache
        in_specs += [
            pl.BlockSpec((None, past, kvw), lambda b: (b, 0, 0)),
            pl.BlockSpec((None, past, kvw), lambda b: (b, 0, 0)),
            pl.BlockSpec((seq, qw), lambda b: (0, 0)),
            pl.BlockSpec((seq, qw), lambda b: (0, 0)),
        ]
        args += [ck, cv, cos, sin]
    else:
        out_specs += [pl.BlockSpec((None, seq, kvw), lambda b: (b, 0, 0))] * 2
        out_shape += [jax.ShapeDtypeStruct((batch, seq, kvw), F32)] * 2
    return pl.pallas_call(
        kern,
        grid=(batch,),
        in_specs=in_specs,
        out_specs=out_specs,
        out_shape=out_shape,
        scratch_shapes=[pltpu.VMEM((seq, qw), BF16), pltpu.VMEM((past + seq, kvw), BF16),
                        pltpu.VMEM((past + seq, kvw), BF16)],
        compiler_params=_cparams(("arbitrary",)),
        name="attention",
    )(*args)


def _rope_tables(seq):
    t = jnp.arange(seq, dtype=jnp.int32)
    row = (t // GRID_W).astype(F32)
    col = (t % GRID_W).astype(F32)
    nf = ATT_HD // 4
    inv = ROPE_THETA ** (-jnp.arange(nf, dtype=F32) / nf)
    a_row = row[:, None] * inv[None, :]
    a_col = col[:, None] * inv[None, :]
    ang = jnp.concatenate([a_row, a_row, a_col, a_col], axis=1)
    sign = jnp.concatenate([-jnp.ones((nf,), F32), jnp.ones((nf,), F32)] * 2)
    cos = jnp.tile(jnp.cos(ang), (1, ATT_HEADS))
    sin = jnp.tile(jnp.sin(ang) * sign[None, :], (1, ATT_HEADS))
    return cos, sin


def _dft_matrices(seq):
    two_l = 2 * seq
    f = jnp.arange(seq, dtype=jnp.int32)
    t = jnp.arange(seq, dtype=jnp.int32)
    m_fwd = (f[:, None] * t[None, :]) % two_l
    ang_fwd = m_fwd.astype(F32) * (math.pi / seq)
    alt_t = jnp.where(t % 2 == 0, 1.0, -1.0).astype(F32)
    fc = jnp.cos(ang_fwd)
    fs = jnp.where(f[:, None] == 0, alt_t[None, :], jnp.sin(ang_fwd))
    n_out = t + seq // 2
    m_inv = (n_out[:, None] * f[None, :]) % two_l
    ang_inv = m_inv.astype(F32) * (math.pi / seq)
    wgt = jnp.where(f == 0, 1.0, 2.0).astype(F32) / two_l
    alt_n = jnp.where(n_out % 2 == 0, 1.0, -1.0).astype(F32)
    gc = jnp.cos(ang_inv) * wgt[None, :]
    gs = jnp.where(f[None, :] == 0, alt_n[:, None] / two_l, jnp.sin(ang_inv) * wgt[None, :])
    out = []
    for m in (fc, fs, gc, gs):
        hi = m.astype(BF16)
        lo = (m - hi.astype(F32)).astype(BF16)
        out += [hi, lo]
    return tuple(out)


def _hyena_feats(seq):
    t = jnp.arange(seq, dtype=F32) / seq
    bands = jnp.arange(1, HY_BANDS + 1, dtype=F32)
    ang = 2.0 * jnp.pi * t[:, None] * bands[None, :]
    z = jnp.concatenate([t[:, None], jnp.sin(ang), jnp.cos(ang)], -1)
    return jnp.pad(z, ((0, 0), (0, LANES - HY_FEAT)))


def _hyena_filter_kernel(z_ref, w1_ref, b1_ref, w2_ref, b2_ref, fr_ref, w3_ref, lr_ref,
                         fch_ref, fcl_ref, fsh_ref, fsl_ref, hc_ref, hs_ref, *, seq):
    fr = fr_ref[...]
    h = jnp.sin(fr[0:1, :] * (_dot3(z_ref[...], w1_ref[...]) + b1_ref[...]))
    h = jnp.sin(fr[1:2, :] * (_dot3(h, w2_ref[...]) + b2_ref[...]))
    h = _dot3(h, w3_ref[...])
    t = lax.broadcasted_iota(jnp.int32, (seq, 1), 0).astype(F32)
    dist = jnp.abs(t - float(seq // 2)) / (0.5 * seq)
    h = h * jnp.exp(-jnp.exp(lr_ref[...]) * dist)
    hc_ref[...] = _dot3_pre(fch_ref[...], fcl_ref[...], h)
    hs_ref[...] = _dot3_pre(fsh_ref[...], fsl_ref[...], h)


def _hyena_filters(z, w1p, b1, w2, b2, fr, w3, lr, dft, seq):
    fch, fcl, fsh, fsl = dft[:4]
    full = lambda a: pl.BlockSpec(a.shape, lambda: (0,) * a.ndim)
    args = [z, w1p, b1, w2, b2, fr, w3, lr, fch, fcl, fsh, fsl]
    return pl.pallas_call(
        functools.partial(_hyena_filter_kernel, seq=seq),
        in_specs=[full(a) for a in args],
        out_specs=[pl.BlockSpec((seq, 2 * HY_C), lambda: (0, 0))] * 2,
        out_shape=[jax.ShapeDtypeStruct((seq, 2 * HY_C), F32)] * 2,
        compiler_params=pltpu.CompilerParams(vmem_limit_bytes=VMEM_LIMIT),
        name="hyena_filters",
    )(*args)


def _shift_rows(u, seq):
    row = lax.broadcasted_iota(jnp.int32, u.shape, 0)
    prev = jnp.where(row == 0, 0.0, pltpu.roll(u, 1, axis=0))
    nxt = jnp.where(row == seq - 1, 0.0, pltpu.roll(u, seq - 1, axis=0))
    return prev, nxt


def _hyena_kernel(v_ref, x1_ref, x2_ref, cv_ref, c1_ref, c2_ref, bias_ref,
                  hc0_ref, hs0_ref, hc1_ref, hs1_ref,
                  fch_ref, fcl_ref, fsh_ref, fsl_ref, gch_ref, gcl_ref, gsh_ref, gsl_ref, o_ref, *, seq):
    def dwconv(u_ref, w_ref):
        u = u_ref[...]
        w = w_ref[...]
        prev, nxt = _shift_rows(u, seq)
        return prev * w[0:1, :] + u * w[1:2, :] + nxt * w[2:3, :]

    row0 = lax.broadcasted_iota(jnp.int32, (seq, COL_TILE), 0) == 0

    def long_conv(u, hc, hs):
        uc = _dot3_pre(fch_ref[...], fcl_ref[...], u)
        us = _dot3_pre(fsh_ref[...], fsl_ref[...], u)
        ss = us * hs
        yc = uc * hc - jnp.where(row0, 0.0, ss)
        ys = jnp.where(row0, ss, uc * hs + us * hc)
        return _dot3_pre(gch_ref[...], gcl_ref[...], yc) + _dot3_pre(gsh_ref[...], gsl_ref[...], ys)

    bias = bias_ref[...]
    v = dwconv(v_ref, cv_ref)
    z = long_conv(v, hc0_ref[...], hs0_ref[...]) + bias[0:1, :] * v
    z = dwconv(x1_ref, c1_ref) * z
    z = long_conv(z, hc1_ref[...], hs1_ref[...]) + bias[1:2, :] * z
    o_ref[...] = (dwconv(x2_ref, c2_ref) * z).astype(BF16)


def _hyena(p, hy_conv, hy_bias, hc, hs, dft, layer, batch, seq):
    n = batch * seq
    nb = HY_C // COL_TILE
    tile = lambda off: pl.BlockSpec((seq, COL_TILE), lambda b, c: (b, T_HY + off * nb + c))
    cw = lambda off: pl.BlockSpec((None, 3, COL_TILE), lambda b, c: (layer, 0, off * nb + c))
    filt = lambda o: pl.BlockSpec((seq, COL_TILE), lambda b, c: (0, o * nb + c))
    mat = pl.BlockSpec((seq, seq), lambda b, c: (0, 0))
    return pl.pallas_call(
        functools.partial(_hyena_kernel, seq=seq),
        grid=(batch, nb),
        in_specs=[tile(0), tile(1), tile(2), cw(0), cw(1), cw(2),
                  pl.BlockSpec((None, 2, COL_TILE), lambda b, c: (layer, 0, c)),
                  filt(0), filt(0), filt(1), filt(1)] + [mat] * 8,
        out_specs=pl.BlockSpec((seq, COL_TILE), lambda b, c: (b, c)),
        out_shape=jax.ShapeDtypeStruct((n, HY_C), BF16),
        compiler_params=_cparams(("arbitrary", "arbitrary")),
        name="hyena",
    )(p, p, p, hy_conv, hy_conv, hy_conv, hy_bias, hc, hs, hc, hs, *dft)


def _stack2(x):
    lane = lax.broadcasted_iota(jnp.int32, x.shape, 1)
    first = lane < RW_N
    return jnp.concatenate([jnp.where(first, x, 0.0), jnp.where(first, 0.0, x)], axis=0)


def _rwkv_kernel(*refs, seq, has_s0):
    (pr_ref, pk_ref, pv_ref, pl_ref, mur_ref, muk_ref, muv_ref, mul_ref, aup_ref, gup_ref, wup_ref, w0_ref,
     a0_ref, kk_ref, ka_ref, rk_ref, bd_ref) = refs[:17]
    pos = 17
    s0_ref = None
    if has_s0:
        s0_ref = refs[pos]
        pos += 1
    o_ref, st_ref = refs[pos], refs[pos + 1]
    r_s, k_s, v_s, a_s, b_s, gate_s, bonus_s, y_scr, e_s, pre_scr = refs[pos + 2:pos + 12]

    def shifted(ref, mu_ref):
        p = ref[...]
        prev, nxt = _shift_rows(p, seq)
        return p + mu_ref[...] * (0.5 * (prev + nxt) - p)

    bd = bd_ref[...]
    r = shifted(pr_ref, mur_ref)
    k = shifted(pk_ref, muk_ref)
    v = shifted(pv_ref, muv_ref)
    low = shifted(pl_ref, mul_ref)
    wa = low[:, :LANES]
    iclr = jax.nn.sigmoid(a0_ref[...] + _dot(wa, aup_ref[...]))
    gate_s[...] = _dot(jax.nn.sigmoid(low[:, LANES:]), gup_ref[...])
    kk = k * kk_ref[...]
    kk = kk * lax.rsqrt(_dot_x2(kk * kk, bd) + 1e-12)
    k2 = k * (1.0 + (iclr - 1.0) * ka_ref[...])
    bonus_s[...] = _dot_x2(r * k2 * rk_ref[...], bd) * v
    r_s[...] = r
    k_s[...] = k2
    v_s[...] = v
    a_s[...] = -kk
    b_s[...] = kk * iclr
    tw = jnp.tanh(wa)
    for d in range(2):
        x = -(w0_ref[d:d + 1, :] + _dot(tw, wup_ref[d]))
        softplus = jnp.maximum(x, 0.0) + jnp.log1p(jnp.exp(-jnp.abs(x)))
        e_s[d] = jnp.exp(-softplus - 0.5)

    C = RW_CHUNK
    S = 2 * C
    nc = seq // C
    ri = lax.broadcasted_iota(jnp.int32, (S, S), 0)
    ci = lax.broadcasted_iota(jnp.int32, (S, S), 1)
    eye = ri == ci
    tr = ri & (C - 1)
    tc = ci & (C - 1)
    ti = lax.broadcasted_iota(jnp.int32, (C, C), 0)
    tj = lax.broadcasted_iota(jnp.int32, (C, C), 1)

    for d in range(2):
        if d == 0:
            strict, incl = tc < tr, tc <= tr
            tri = (tj <= ti).astype(BF16)
            last = C - 1
        else:
            strict, incl = tc > tr, tc >= tr
            tri = (tj >= ti).astype(BF16)
            last = 0

        def phase1(c, carry, d=d, strict=strict, incl=incl, tri=tri, last=last):
            r0 = pl.multiple_of(c * C, C)
            rows = pl.ds(r0, C)
            ec = e_s[d, rows, :]
            cum = _dot_m2x(tri, ec)
            g_c = cum[last:last + 1, :]
            dec = jnp.exp(-cum)
            inv = jnp.exp(cum)
            to_end = jnp.exp(cum - g_c)
            a_t = _stack2(a_s[rows, :] * jnp.exp(ec - cum))
            r_t = _stack2(r_s[rows, :] * dec)
            bc = b_s[rows, :]
            kc = k_s[rows, :]
            b_t = _stack2(bc * inv)
            k_t = _stack2(kc * inv)
            b_h = _stack2(bc * to_end)
            k_h = _stack2(kc * to_end)
            v_st = _stack2(v_s[rows, :])
            gm = _dot_nt(jnp.concatenate([a_t, r_t], axis=0), jnp.concatenate([b_t, k_t], axis=0))
            n_m = jnp.where(strict, gm[:S, :S], 0.0)
            ak = jnp.where(strict, gm[:S, S:], 0.0)
            rb = jnp.where(incl, gm[S:, :S], 0.0)
            rk = jnp.where(incl, gm[S:, S:], 0.0)
            t_m = jnp.where(eye, 1.0, 0.0)
            m = 1
            while m < C:
                lg = m.bit_length() - 1
                same = jnp.right_shift(tr, lg + 1) == jnp.right_shift(tc, lg + 1)
                halves = (jnp.right_shift(tr, lg) & 1) != (jnp.right_shift(tc, lg) & 1)
                n_off = jnp.where(same & halves, n_m, 0.0)
                if m == 1:
                    t_m = t_m + n_off
                else:
                    t_m = t_m + _dot(t_m, _dot(n_off, t_m))
                m *= 2
            akv = _dot(ak, v_st)
            wu = _dot(t_m, jnp.concatenate([a_t, akv], axis=1))
            w_t = wu[:, :S]
            u_t = wu[:, S:]
            uv = jnp.concatenate([u_t, v_st], axis=0)
            pre_scr[c, 0] = r_t + _dot(rb, w_t)
            pre_scr[c, 1] = _dot(jnp.concatenate([rb, rk], axis=1), uv)
            pre_scr[c, 2] = jnp.where(eye, jnp.exp(-g_c), 0.0) + _dot_tn(w_t, b_h)
            pre_scr[c, 3] = _dot_tn(uv, jnp.concatenate([b_h, k_h], axis=0))
            return carry

        lax.fori_loop(0, nc, phase1, 0)

        def phase2(i, carry, d=d):
            c = i if d == 0 else nc - 1 - i
            r0 = pl.multiple_of(c * C, C)
            s = st_ref[d]
            y_st = _dot_nt(pre_scr[c, 0], s) + pre_scr[c, 1]
            y = y_st[:C, :] + y_st[C:, :]
            if d == 0:
                y_scr[pl.ds(r0, C), :] = y
            else:
                y_scr[pl.ds(r0, C), :] = y_scr[pl.ds(r0, C), :] + y
            st_ref[d] = _dot(s, pre_scr[c, 2]) + pre_scr[c, 3]
            return carry

        st_ref[d] = s0_ref[d] if has_s0 else jnp.zeros((S, S), F32)
        lax.fori_loop(0, nc, phase2, 0)

    y = y_scr[...]
    ones_bd = bd_ref[...]
    mean = _dot_x2(y, ones_bd) * (1.0 / RW_N)
    yc = y - mean
    var = _dot_x2(yc * yc, ones_bd) * (1.0 / RW_N)
    yn = yc * lax.rsqrt(var + RW_LN_EPS)
    o_ref[...] = ((yn + bonus_s[...]) * gate_s[...]).astype(BF16)


def _rwkv(p, mu, aup, gup, wup, w0, a0, kk, ka, rk, bd2, s0, batch, seq):
    n = batch * seq
    npair = RW_HEADS // 2
    has_s0 = s0 is not None
    S = 2 * RW_CHUNK
    base = T_RW * COL_TILE // LANES
    per = RW_C // LANES
    pcol = lambda sec: pl.BlockSpec((seq, LANES), lambda b, h: (b, base + sec * per + h))
    mucol = lambda sec: pl.BlockSpec((1, LANES), lambda b, h: (0, sec * per + h))
    vec = pl.BlockSpec((1, LANES), lambda b, h: (0, h))
    low_idx = 3 * RW_C // COL_TILE
    in_specs = [
        pcol(0), pcol(1), pcol(2),
        pl.BlockSpec((seq, COL_TILE), lambda b, h: (b, T_RW + low_idx)),
        mucol(0), mucol(1), mucol(2),
        pl.BlockSpec((1, COL_TILE), lambda b, h: (0, low_idx)),
        pl.BlockSpec((LANES, LANES), lambda b, h: (0, h)),
        pl.BlockSpec((LANES, LANES), lambda b, h: (0, h)),
        pl.BlockSpec((2, LANES, LANES), lambda b, h: (0, 0, h)),
        pl.BlockSpec((2, LANES), lambda b, h: (0, h)),
        vec, vec, vec, vec,
        pl.BlockSpec((LANES, LANES), lambda b, h: (0, 0)),
    ]
    args = [p, p, p, p, mu, mu, mu, mu, aup, gup, wup, w0, a0, kk, ka, rk, bd2]
    if has_s0:
        in_specs.append(pl.BlockSpec((None, 2, None, S, S), lambda b, h: (b, 0, h, 0, 0)))
        args.append(s0)
    seq_buf = pltpu.VMEM((seq, LANES), F32)
    return pl.pallas_call(
        functools.partial(_rwkv_kernel, seq=seq, has_s0=has_s0),
        grid=(batch, npair),
        in_specs=in_specs,
        out_specs=[pl.BlockSpec((seq, LANES), lambda b, h: (b, h)),
                   pl.BlockSpec((None, 2, None, S, S), lambda b, h: (b, 0, h, 0, 0))],
        out_shape=[jax.ShapeDtypeStruct((n, RW_C), BF16), jax.ShapeDtypeStruct((batch, 2, npair, S, S), F32)],
        scratch_shapes=[seq_buf] * 8 + [pltpu.VMEM((2, seq, LANES), F32),
                                        pltpu.VMEM((seq // RW_CHUNK, 4, S, S), F32)],
        compiler_params=_cparams(("arbitrary", "arbitrary")),
        name="rwkv7",
    )(*args)


def _pair_states(s):
    b = s.shape[0]
    s = s.reshape(b, 2, RW_HEADS // 2, 2, RW_N, RW_N)
    z = jnp.zeros_like(s[:, :, :, 0])
    top = jnp.concatenate([s[:, :, :, 0], z], axis=-1)
    bot = jnp.concatenate([z, s[:, :, :, 1]], axis=-1)
    return jnp.concatenate([top, bot], axis=-2)


def _unpair_states(s):
    b = s.shape[0]
    h0 = s[:, :, :, :RW_N, :RW_N]
    h1 = s[:, :, :, RW_N:, RW_N:]
    return jnp.stack([h0, h1], axis=3).reshape(b, 2, RW_HEADS, RW_N, RW_N)


MRG_TM = 1024
MRG_TN = 256


def _merge_kernel(h_ref, ba_ref, bb_ref, bc_ref, bd_ref, g0_ref, g1_ref, g2_ref, g3_ref, wb_ref, o_ref):
    h = h_ref[...]
    acc = None
    for i, (br, gw) in enumerate(((ba_ref, g0_ref), (bb_ref, g1_ref), (bc_ref, g2_ref), (bd_ref, g3_ref))):
        gate = jax.nn.sigmoid(jnp.dot(h, gw[...].astype(BF16), preferred_element_type=F32))
        proj = jnp.dot(br[...], wb_ref[i].astype(BF16), preferred_element_type=F32)
        acc = gate * proj if acc is None else acc + gate * proj
    o_ref[...] = acc.astype(BF16)


def _merge(h, branches, w_in, w_branch, layer):
    n = h.shape[0]
    gate_spec = lambda b: pl.BlockSpec(
        (None, D_MODEL, MRG_TN), lambda i, j: (layer, 0, (MIX_W + b * D_MODEL) // MRG_TN + j))
    return pl.pallas_call(
        _merge_kernel,
        grid=(n // MRG_TM, D_MODEL // MRG_TN),
        in_specs=[pl.BlockSpec((MRG_TM, D_MODEL), lambda i, j: (i, 0))]
        + [pl.BlockSpec((MRG_TM, BRANCH_W), lambda i, j: (i, 0))] * N_BRANCH
        + [gate_spec(b) for b in range(N_BRANCH)]
        + [pl.BlockSpec((None, N_BRANCH, BRANCH_W, MRG_TN), lambda i, j: (layer, 0, 0, j))],
        out_specs=pl.BlockSpec((MRG_TM, MRG_TN), lambda i, j: (i, j)),
        out_shape=jax.ShapeDtypeStruct((n, D_MODEL), BF16),
        compiler_params=_cparams(("arbitrary", "arbitrary")),
        name="gated_merge",
    )(h, *branches, w_in, w_in, w_in, w_in, w_branch)


OUT_TM = 512
OUT_TK = 512


def _route(glog, elog):
    lane_i = lax.broadcasted_iota(jnp.int32, glog.shape, 1)
    lane = lane_i.astype(F32)
    grp_of_lane = jnp.right_shift(lane_i, MOE_PER_GROUP.bit_length() - 1).astype(F32)
    neg = -jnp.inf
    gl = jnp.where(lane_i < MOE_GROUPS, glog, neg)
    gmax = jnp.max(gl, axis=-1, keepdims=True)
    gsel = jnp.min(jnp.where(gl == gmax, lane, float(LANES)), axis=-1, keepdims=True)
    gval = 1.0 / jnp.sum(jnp.exp(gl - gmax), axis=-1, keepdims=True)
    in_grp = (grp_of_lane == gsel) & (lane_i < MOE_EXPERTS)
    el = jnp.where(in_grp, elog, neg)
    ee = jnp.exp(el - jnp.max(el, axis=-1, keepdims=True))
    prob = jnp.where(in_grp, ee / jnp.sum(ee, axis=-1, keepdims=True), -1.0)
    v1 = jnp.max(prob, axis=-1, keepdims=True)
    i1 = jnp.min(jnp.where(prob == v1, lane, float(LANES)), axis=-1, keepdims=True)
    prob2 = jnp.where(lane == i1, -1.0, prob)
    v2 = jnp.max(prob2, axis=-1, keepdims=True)
    i2 = jnp.min(jnp.where(prob2 == v2, lane, float(LANES)), axis=-1, keepdims=True)
    tot = v1 + v2
    return jnp.where(lane == i1, gval * (v1 / tot), 0.0) + jnp.where(lane == i2, gval * (v2 / tot), 0.0)


def _outproj_kernel(m_ref, w_ref, x_ref, mod_ref, lng_ref, lnb_ref, rgw_ref, rgb_ref, rew_ref, reb_ref,
                    x1_ref, h2_ref, comb_ref, acc_ref, *, req_base, rows_per_req):
    i = pl.program_id(0)
    k = pl.program_id(1)

    @pl.when(k == 0)
    def _():
        acc_ref[...] = jnp.zeros_like(acc_ref)

    acc_ref[...] += jnp.dot(m_ref[...], w_ref[...].astype(BF16), preferred_element_type=F32)

    @pl.when(k == pl.num_programs(1) - 1)
    def _():
        m = req_base + (i * OUT_TM) // rows_per_req
        g1 = mod_ref[2, pl.ds(m, 1), :]
        sh2 = mod_ref[3, pl.ds(m, 1), :]
        sc2 = mod_ref[4, pl.ds(m, 1), :]
        x1 = _ln(ALPHA * x_ref[...] + g1 * acc_ref[...]) * lng_ref[0:1, :] + lnb_ref[0:1, :]
        x1_ref[...] = x1
        h2 = _ln(x1) * (1.0 + sc2) + sh2
        h2_ref[...] = h2.astype(BF16)
        glog = _dot3(h2, rgw_ref[...]) + rgb_ref[...]
        elog = _dot3(h2, rew_ref[...]) + reb_ref[...]
        comb_ref[...] = _route(glog, elog)


def _outproj(merged, w_out, x2d, mod_l, ln_g, ln_b, rgw, rgb, rew, reb, layer, req_base, rows_per_req):
    n = x2d.shape[0]
    kern = functools.partial(_outproj_kernel, req_base=req_base, rows_per_req=rows_per_req)
    row = lambda w: pl.BlockSpec((OUT_TM, w), lambda i, k: (i, 0))
    full = lambda a: pl.BlockSpec(a.shape, lambda i, k: (0,) * a.ndim)
    lnspec = pl.BlockSpec((None, 2, D_MODEL), lambda i, k: (layer, 0, 0))
    return pl.pallas_call(
        kern,
        grid=(n // OUT_TM, D_MODEL // OUT_TK),
        in_specs=[pl.BlockSpec((OUT_TM, OUT_TK), lambda i, k: (i, k)),
                  pl.BlockSpec((None, OUT_TK, D_MODEL), lambda i, k: (layer, k, 0)),
                  row(D_MODEL), full(mod_l), lnspec, lnspec, full(rgw), full(rgb), full(rew), full(reb)],
        out_specs=[row(D_MODEL), row(D_MODEL), row(LANES)],
        out_shape=[jax.ShapeDtypeStruct((n, D_MODEL), F32), jax.ShapeDtypeStruct((n, D_MODEL), BF16),
                   jax.ShapeDtypeStruct((n, LANES), F32)],
        scratch_shapes=[pltpu.VMEM((OUT_TM, D_MODEL), F32)],
        compiler_params=_cparams(("arbitrary", "arbitrary")),
        name="out_projection",
    )(merged, w_out, x2d, mod_l, ln_g, ln_b, rgw, rgb, rew, reb)


MOE_TM = 1024


def _moe_kernel(h_ref, comb_ref, wg_ref, wu_ref, wd_ref, o_ref):
    e = pl.program_id(1)

    @pl.when(e == 0)
    def _():
        o_ref[...] = jnp.zeros_like(o_ref)

    h = h_ref[...]
    comb = comb_ref[...]
    lane = lax.broadcasted_iota(jnp.int32, comb.shape, 1)
    ce = jnp.sum(jnp.where(lane == e, comb, 0.0), axis=-1, keepdims=True)
    hg = jnp.dot(h, wg_ref[...].astype(BF16), preferred_element_type=F32)
    hu = jnp.dot(h, wu_ref[...].astype(BF16), preferred_element_type=F32)
    act = _silu(hg) * hu * ce
    o_ref[...] += jnp.dot(act.astype(BF16), wd_ref[...].astype(BF16), preferred_element_type=F32)


def _moe(h2, comb, w_gate, w_up, w_down, layer):
    n = h2.shape[0]
    return pl.pallas_call(
        _moe_kernel,
        grid=(n // MOE_TM, MOE_EXPERTS),
        in_specs=[pl.BlockSpec((MOE_TM, D_MODEL), lambda i, e: (i, 0)),
                  pl.BlockSpec((MOE_TM, LANES), lambda i, e: (i, 0)),
                  pl.BlockSpec((None, None, D_MODEL, MOE_HID), lambda i, e: (layer, e, 0, 0)),
                  pl.BlockSpec((None, None, D_MODEL, MOE_HID), lambda i, e: (layer, e, 0, 0)),
                  pl.BlockSpec((None, None, MOE_HID, D_MODEL), lambda i, e: (layer, e, 0, 0))],
        out_specs=pl.BlockSpec((MOE_TM, D_MODEL), lambda i, e: (i, 0)),
        out_shape=jax.ShapeDtypeStruct((n, D_MODEL), F32),
        compiler_params=_cparams(("arbitrary", "arbitrary")),
        name="moe_experts",
    )(h2, comb, w_gate, w_up, w_down)


FIN_TM = 512


def _final_kernel(x_ref, f_ref, mod_ref, lng_ref, lnb_ref, o_ref, *, req_base, rows_per_req):
    i = pl.program_id(0)
    m = req_base + (i * FIN_TM) // rows_per_req
    g2 = mod_ref[5, pl.ds(m, 1), :]
    o_ref[...] = _ln(ALPHA * x_ref[...] + g2 * f_ref[...]) * lng_ref[1:2, :] + lnb_ref[1:2, :]


def _final(x1, ffn, mod_l, ln_g, ln_b, layer, req_base, rows_per_req):
    n = x1.shape[0]
    row = pl.BlockSpec((FIN_TM, D_MODEL), lambda i: (i, 0))
    lnspec = pl.BlockSpec((None, 2, D_MODEL), lambda i: (layer, 0, 0))
    return pl.pallas_call(
        functools.partial(_final_kernel, req_base=req_base, rows_per_req=rows_per_req),
        grid=(n // FIN_TM,),
        in_specs=[row, row, pl.BlockSpec(mod_l.shape, lambda i: (0, 0, 0)), lnspec, lnspec],
        out_specs=row,
        out_shape=jax.ShapeDtypeStruct((n, D_MODEL), F32),
        compiler_params=_cparams(("arbitrary",)),
        name="final_norm",
    )(x1, ffn, mod_l, ln_g, ln_b)


def _block_diag_ones(width, blk):
    i = jnp.arange(width) // blk
    return (i[:, None] == i[None, :]).astype(BF16)


def _layer(x2d, layer, batch, seq, req_base, rows_per_req, mod_l, wts, consts, ctx):
    decode = ctx is not None
    p, h = _inproj(x2d, mod_l, wts['w_in'], layer, req_base, rows_per_req)

    ret = _retention(p, consts['ret_dec'][layer], ctx['ret'] if decode else None, batch, seq,
                     emit_state=not decode)
    cache = (ctx['k'], ctx['v'], consts['rope_cos'], consts['rope_sin']) if decode else None
    att = _attention(p, consts['gq'][layer], consts['gk'][layer], consts['bd_q'], consts['bd_k'],
                     batch, seq, cache)
    hc, hs = consts['hy_filt'][seq][layer]
    o_c = _hyena(p, wts['hy_conv'], wts['hy_bias'], hc, hs, consts['dft'][seq], layer, batch, seq)
    o_d, rw_state = _rwkv(p, consts['rw_mu'][layer], consts['rw_aup'][layer], wts['rw_g_up'][layer],
                          consts['rw_wup'][layer], wts['rw_w0'][layer], consts['rw_a0'][layer],
                          consts['rw_kk'][layer], consts['rw_ka'][layer], consts['rw_rk'][layer],
                          consts['bd_pair'], ctx['rw'] if decode else None, batch, seq)

    merged = _merge(h, (ret[0], att[0], o_c, o_d), wts['w_in'], wts['w_branch'], layer)
    x1, h2, comb = _outproj(merged, wts['w_out'], x2d, mod_l, wts['ln_g'], wts['ln_b'],
                            consts['rg_w'][layer], consts['rg_b'][layer], consts['re_w'][layer],
                            consts['re_b'][layer], layer, req_base, rows_per_req)
    ffn = _moe(h2, comb, wts['moe_w_gate'], wts['moe_w_up'], wts['moe_w_down'], layer)
    x2 = _final(x1, ffn, mod_l, wts['ln_g'], wts['ln_b'], layer, req_base, rows_per_req)
    if decode:
        return x2, None
    return x2, (att[1], att[2], ret[1], rw_state)


def kernel(x_prompt, x_sample, c, cache_attn_k, cache_attn_v, state_ret, state_rwkv, c_ctx, mod_w, mod_b, w_in, ret_decay_exp, attn_q_norm, attn_k_norm, hy_conv, hy_w1, hy_b1, hy_w2, hy_b2, hy_freq, hy_w3, hy_log_rate, hy_bias, rw_mu, rw_w0, rw_w_up, rw_a0, rw_a_up, rw_g_up, rw_k_k, rw_k_a, rw_r_k, w_branch, w_out, ln_g, ln_b, moe_rg_w, moe_rg_b, moe_re_w, moe_re_b, moe_w_gate, moe_w_up, moe_w_down):
    batch, seq, _ = x_prompt.shape
    dbatch, dseq, _ = x_sample.shape
    past = cache_attn_k.shape[2]
    kvw = ATT_KV_HEADS * ATT_HD

    cvec = jnp.concatenate([c_ctx[None, :], c, jnp.zeros((8 - 1 - dbatch, D_MODEL), F32)], axis=0)
    mod = _modulation(cvec, mod_w, mod_b)

    wts = dict(w_in=w_in, hy_conv=hy_conv, hy_bias=hy_bias, rw_g_up=rw_g_up, rw_w0=rw_w0, w_branch=w_branch,
               w_out=w_out, ln_g=ln_g, ln_b=ln_b, moe_w_gate=moe_w_gate, moe_w_up=moe_w_up,
               moe_w_down=moe_w_down)

    zlo = jnp.zeros((DEPTH, RW_N, RW_C), F32)
    rope_cos, rope_sin = _rope_tables(dseq)
    consts = dict(
        ret_dec=jnp.repeat(ret_decay_exp, RET_DV, axis=-1),
        gq=jnp.tile(attn_q_norm, (1, ATT_HEADS))[:, None, :],
        gk=jnp.tile(attn_k_norm, (1, ATT_KV_HEADS))[:, None, :],
        bd_q=_block_diag_ones(ATT_HEADS * ATT_HD, ATT_HD),
        bd_k=_block_diag_ones(kvw, ATT_HD),
        bd_pair=_block_diag_ones(LANES, RW_N),
        rope_cos=rope_cos, rope_sin=rope_sin,
        rw_mu=rw_mu[:, None, :],
        rw_aup=jnp.concatenate([zlo, rw_a_up], axis=1),
        rw_wup=jnp.concatenate([rw_w_up, jnp.zeros_like(rw_w_up)], axis=2),
        rw_a0=rw_a0[:, None, :], rw_kk=rw_k_k[:, None, :], rw_ka=rw_k_a[:, None, :],
        rw_rk=rw_r_k.reshape(DEPTH, 1, RW_C),
        rg_w=jnp.pad(moe_rg_w, ((0, 0), (0, 0), (0, LANES - MOE_GROUPS))),
        rg_b=jnp.pad(moe_rg_b, ((0, 0), (0, LANES - MOE_GROUPS)))[:, None, :],
        re_w=jnp.pad(moe_re_w, ((0, 0), (0, 0), (0, LANES - MOE_EXPERTS))),
        re_b=jnp.pad(moe_re_b, ((0, 0), (0, LANES - MOE_EXPERTS)))[:, None, :],
        dft={}, hy_filt={},
    )
    w1p = jnp.pad(hy_w1, ((0, 0), (0, LANES - HY_FEAT), (0, 0)))
    for s in sorted({seq, dseq}):
        dft = _dft_matrices(s)
        z = _hyena_feats(s)
        consts['dft'][s] = dft
        consts['hy_filt'][s] = [
            _hyena_filters(z, w1p[l], hy_b1[l][None, :], hy_w2[l], hy_b2[l][None, :], hy_freq[l], hy_w3[l],
                           hy_log_rate[l].reshape(1, 2 * HY_C), dft, s)
            for l in range(DEPTH)]

    y = x_prompt.reshape(batch * seq, D_MODEL)
    ks, vs, rets, rws = [], [], [], []
    for l in range(DEPTH):
        y, (k_l, v_l, ret_l, rw_l) = _layer(y, l, batch, seq, 0, batch * seq, mod[l], wts, consts, None)
        ks.append(k_l)
        vs.append(v_l)
        rets.append(ret_l)
        rws.append(_unpair_states(rw_l))
    y_prompt = y.reshape(batch, seq, D_MODEL)
    new_k = jnp.stack(ks, 1).reshape(batch, DEPTH, seq, ATT_KV_HEADS, ATT_HD)
    new_v = jnp.stack(vs, 1).reshape(batch, DEPTH, seq, ATT_KV_HEADS, ATT_HD)
    new_ret = jnp.stack(rets, 1)
    new_rw = jnp.stack(rws, 1)

    ys = x_sample.reshape(dbatch * dseq, D_MODEL)
    for l in range(DEPTH):
        ctx = dict(k=cache_attn_k[:, l].reshape(dbatch, past, kvw), v=cache_attn_v[:, l].reshape(dbatch, past, kvw),
                   ret=state_ret[:, l], rw=_pair_states(state_rwkv[:, l]))
        ys, _ = _layer(ys, l, dbatch, dseq, 1, dseq, mod[l], wts, consts, ctx)
    y_sample = ys.reshape(dbatch, dseq, D_MODEL)

    return (y_prompt, y_sample, new_k, new_v, new_ret, new_rw)
```

```python
import functools
import math

import jax
import jax.numpy as jnp
from jax import lax
from jax.experimental import pallas as pl
from jax.experimental.pallas import tpu as pltpu

F32 = jnp.float32
BF16 = jnp.bfloat16

D_MODEL = 2048
DEPTH = 4
GRID_W = 64
RET_CHUNK = 128
EPS = 1e-5
RW_LN_EPS = 64e-5
RET_HEADS = 4
RET_DK = 128
RET_DV = 128
ATT_HEADS = 8
ATT_KV_HEADS = 2
ATT_HD = 64
ROPE_THETA = 10000.0
HY_C = 512
HY_BANDS = 8
HY_FEAT = 1 + 2 * HY_BANDS
HY_HID = 64
RW_HEADS = 8
RW_N = 64
RW_C = RW_HEADS * RW_N
RW_CHUNK = 64
N_BRANCH = 4
BRANCH_W = 512
MOE_GROUPS = 4
MOE_PER_GROUP = 8
MOE_EXPERTS = MOE_GROUPS * MOE_PER_GROUP
MOE_HID = 256
N_MOD = 6
ALPHA = (2.0 * DEPTH) ** 0.25

MIX_W = 6144
COL_TILE = 256
LANES = 128
VMEM_LIMIT = 56 * 1024 * 1024

T_RQ, T_RK, T_RV, T_RG = 0, 2, 4, 6
T_AQ, T_AKV = 8, 10
T_HY = 11
T_RW = 17


def _cparams(sem):
    return pltpu.CompilerParams(dimension_semantics=sem, vmem_limit_bytes=VMEM_LIMIT)


def _dot(a, b):
    return jnp.dot(a.astype(BF16), b.astype(BF16), preferred_element_type=F32)


def _dot_nt(a, b):
    return lax.dot_general(a.astype(BF16), b.astype(BF16), (((1,), (1,)), ((), ())), preferred_element_type=F32)


def _dot_tn(a, b):
    return lax.dot_general(a.astype(BF16), b.astype(BF16), (((0,), (0,)), ((), ())), preferred_element_type=F32)


def _split(x):
    hi = x.astype(BF16)
    lo = (x - hi.astype(F32)).astype(BF16)
    return hi, lo


def _dot_x2(x, m):
    hi, lo = _split(x)
    return jnp.dot(hi, m, preferred_element_type=F32) + jnp.dot(lo, m, preferred_element_type=F32)


def _dot_m2x(m, x):
    hi, lo = _split(x)
    return jnp.dot(m, hi, preferred_element_type=F32) + jnp.dot(m, lo, preferred_element_type=F32)


def _dot3(x, y):
    xh, xl = _split(x)
    yh, yl = _split(y)
    return (jnp.dot(xh, yh, preferred_element_type=F32) + jnp.dot(xl, yh, preferred_element_type=F32)
            + jnp.dot(xh, yl, preferred_element_type=F32))


def _dot3_pre(mh, ml, x):
    xh, xl = _split(x)
    return (jnp.dot(mh, xh, preferred_element_type=F32) + jnp.dot(mh, xl, preferred_element_type=F32)
            + jnp.dot(ml, xh, preferred_element_type=F32))


def _ln(x, eps=EPS):
    mu = jnp.mean(x, axis=-1, keepdims=True)
    xc = x - mu
    var = jnp.mean(xc * xc, axis=-1, keepdims=True)
    return xc * lax.rsqrt(var + eps)


def _silu(x):
    return x * jax.nn.sigmoid(x)


MOD_TN = 1024


def _mod_kernel(c_ref, w_ref, b_ref, o_ref):
    s = _silu(c_ref[...])
    o_ref[...] = _dot(s, w_ref[...]) + b_ref[...]


def _modulation(cvec, mod_w, mod_b):
    per = D_MODEL // MOD_TN
    return pl.pallas_call(
        _mod_kernel,
        grid=(DEPTH, N_MOD * per),
        in_specs=[
            pl.BlockSpec((8, D_MODEL), lambda l, n: (0, 0)),
            pl.BlockSpec((None, D_MODEL, MOD_TN), lambda l, n: (l, 0, n)),
            pl.BlockSpec((None, 1, MOD_TN), lambda l, n: (l, 0, n)),
        ],
        out_specs=pl.BlockSpec((None, None, 8, MOD_TN), lambda l, n: (l, n // per, 0, n % per)),
        out_shape=jax.ShapeDtypeStruct((DEPTH, N_MOD, 8, D_MODEL), F32),
        compiler_params=_cparams(("arbitrary", "arbitrary")),
        name="modulation",
    )(cvec, mod_w, mod_b.reshape(DEPTH, 1, N_MOD * D_MODEL))


INP_TM = 1024
INP_TN = 512


def _inproj_kernel(x_ref, mod_ref, w_ref, p_ref, h_ref, *, req_base, rows_per_req):
    i = pl.program_id(0)
    j = pl.program_id(1)

    @pl.when(j == 0)
    def _():
        m = req_base + (i * INP_TM) // rows_per_req
        sh = mod_ref[0, pl.ds(m, 1), :]
        sc = mod_ref[1, pl.ds(m, 1), :]
        h_ref[...] = (_ln(x_ref[...]) * (1.0 + sc) + sh).astype(BF16)

    p_ref[...] = jnp.dot(h_ref[...], w_ref[...].astype(BF16), preferred_element_type=F32)


def _inproj(x2d, mod_l, w_in, layer, req_base, rows_per_req):
    n = x2d.shape[0]
    kern = functools.partial(_inproj_kernel, req_base=req_base, rows_per_req=rows_per_req)
    return pl.pallas_call(
        kern,
        grid=(n // INP_TM, MIX_W // INP_TN),
        in_specs=[
            pl.BlockSpec((INP_TM, D_MODEL), lambda i, j: (i, 0)),
            pl.BlockSpec((N_MOD, 8, D_MODEL), lambda i, j: (0, 0, 0)),
            pl.BlockSpec((None, D_MODEL, INP_TN), lambda i, j: (layer, 0, j)),
        ],
        out_specs=[
            pl.BlockSpec((INP_TM, INP_TN), lambda i, j: (i, j)),
            pl.BlockSpec((INP_TM, D_MODEL), lambda i, j: (i, 0)),
        ],
        out_shape=[jax.ShapeDtypeStruct((n, MIX_W), F32), jax.ShapeDtypeStruct((n, D_MODEL), BF16)],
        compiler_params=_cparams(("arbitrary", "arbitrary")),
        name="in_projection",
    )(x2d, mod_l, w_in)


def _retention_kernel(*refs, seq, has_s0, emit_state):
    q_ref, k_ref, v_ref, g_ref, dec_ref = refs[:5]
    pos = 5
    s0_ref = None
    if has_s0:
        s0_ref = refs[pos]
        pos += 1
    o_ref = refs[pos]
    pos += 1
    st_ref = None
    if emit_state:
        st_ref = refs[pos]
        pos += 1
    s_scr, o_scr = refs[pos], refs[pos + 1]

    C = RET_CHUNK
    nc = seq // C
    lg_all = jnp.log1p(-jnp.exp2(-dec_ref[...]))
    ii = lax.broadcasted_iota(jnp.int32, (C, C), 0)
    jj = lax.broadcasted_iota(jnp.int32, (C, C), 1)
    rel = (ii - jj).astype(F32)
    icol = lax.broadcasted_iota(jnp.int32, (C, 1), 0).astype(F32)

    for d in range(2):
        for h in range(RET_HEADS):
            if has_s0:
                s_scr[d, h] = s0_ref[d, h]
            else:
                s_scr[d, h] = jnp.zeros((RET_DK, RET_DV), F32)

    for d in range(2):
        consts = []
        for h in range(RET_HEADS):
            lg = lg_all[d:d + 1, h * RET_DV:(h + 1) * RET_DV]
            lg1 = lg[:, :1]
            if d == 0:
                dmask = jnp.where(rel >= 0, jnp.exp(lg * rel), 0.0)
                q_dec = jnp.exp(lg1 * (icol + 1.0))
                k_dec = jnp.exp(lg1 * (C - 1.0 - icol))
            else:
                dmask = jnp.where(rel <= 0, jnp.exp(lg * (-rel)), 0.0)
                q_dec = jnp.exp(lg1 * (C - icol))
                k_dec = jnp.exp(lg1 * icol)
            consts.append((dmask, q_dec, k_dec, jnp.exp(lg * float(C))))

        def body(ci, carry, d=d, consts=consts):
            c = ci if d == 0 else nc - 1 - ci
            r0 = pl.multiple_of(c * C, C)
            for h in range(RET_HEADS):
                dmask, q_dec, k_dec, c_dec = consts[h]
                cs = slice(h * RET_DK, (h + 1) * RET_DK)
                qc = q_ref[pl.ds(r0, C), cs]
                kc = k_ref[pl.ds(r0, C), cs] * (RET_DK ** -0.5)
                vc = v_ref[pl.ds(r0, C), cs]
                s = s_scr[d, h]
                att = _dot_nt(qc, kc) * dmask
                out = _dot(att, vc) + _dot(qc, s) * q_dec
                s_scr[d, h] = s * c_dec + _dot_tn(kc * k_dec, vc)
                if d == 0:
                    o_scr[pl.ds(r0, C), cs] = out
                else:
                    o_scr[pl.ds(r0, C), cs] = o_scr[pl.ds(r0, C), cs] + out
            return carry

        lax.fori_loop(0, nc, body, 0)

    for h in range(RET_HEADS):
        cs = slice(h * RET_DV, (h + 1) * RET_DV)
        o_ref[:, cs] = (_ln(o_scr[:, cs]) * _silu(g_ref[:, cs])).astype(BF16)
    if emit_state:
        st_ref[...] = s_scr[...]


def _retention(p, dec_rep, s0, batch, seq, emit_state):
    n = batch * seq
    w = RET_HEADS * RET_DK
    has_s0 = s0 is not None
    kern = functools.partial(_retention_kernel, seq=seq, has_s0=has_s0, emit_state=emit_state)
    in_specs = [
        pl.BlockSpec((seq, w), lambda b: (b, 0)),
        pl.BlockSpec((seq, w), lambda b: (b, 1)),
        pl.BlockSpec((seq, w), lambda b: (b, 2)),
        pl.BlockSpec((seq, w), lambda b: (b, 3)),
        pl.BlockSpec((2, w), lambda b: (0, 0)),
    ]
    args = [p, p, p, p, dec_rep]
    if has_s0:
        in_specs.append(pl.BlockSpec((None, 2, RET_HEADS, RET_DK, RET_DV), lambda b: (b, 0, 0, 0, 0)))
        args.append(s0)
    out_specs = [pl.BlockSpec((seq, w), lambda b: (b, 0))]
    out_shape = [jax.ShapeDtypeStruct((n, w), BF16)]
    if emit_state:
        out_specs.append(pl.BlockSpec((None, 2, RET_HEADS, RET_DK, RET_DV), lambda b: (b, 0, 0, 0, 0)))
        out_shape.append(jax.ShapeDtypeStruct((batch, 2, RET_HEADS, RET_DK, RET_DV), F32))
    return pl.pallas_call(
        kern,
        grid=(batch,),
        in_specs=in_specs,
        out_specs=out_specs,
        out_shape=out_shape,
        scratch_shapes=[pltpu.VMEM((2, RET_HEADS, RET_DK, RET_DV), F32), pltpu.VMEM((seq, w), F32)],
        compiler_params=_cparams(("arbitrary",)),
        name="retention",
    )(*args)


ATT_QB = 256


def _head_rms(x, ones_bd, gain):
    ss = _dot_x2(x * x, ones_bd)
    return x * lax.rsqrt(ss * (1.0 / ATT_HD) + EPS) * gain


def _rope(x, cos, sin_signed):
    w = x.shape[-1]
    lane = lax.broadcasted_iota(jnp.int32, x.shape, 1)
    first = (lane & 16) == 0
    swapped = jnp.where(first, pltpu.roll(x, w - 16, axis=1), pltpu.roll(x, 16, axis=1))
    return x * cos + swapped * sin_signed


def _attention_kernel(*refs, seq, decode, past):
    q_ref, kv_ref, gq_ref, gk_ref, bdq_ref, bdk_ref = refs[:6]
    pos = 6
    if decode:
        ck_ref, cv_ref, cos_ref, sin_ref = refs[pos:pos + 4]
        pos += 4
    o_ref = refs[pos]
    pos += 1
    if not decode:
        ko_ref, vo_ref = refs[pos:pos + 2]
        pos += 2
    q_scr, k_scr, v_scr = refs[pos:pos + 3]

    kvw = ATT_KV_HEADS * ATT_HD
    q = _head_rms(q_ref[...], bdq_ref[...], gq_ref[...])
    kv = kv_ref[...]
    k = _head_rms(kv[:, :kvw], bdk_ref[...], gk_ref[...])
    v = kv[:, kvw:]
    if decode:
        cos = cos_ref[...]
        sin = sin_ref[...]
        q = _rope(q, cos, sin)
        k = _rope(k, cos[:, :kvw], sin[:, :kvw])
        k_scr[0:past, :] = ck_ref[...].astype(BF16)
        v_scr[0:past, :] = cv_ref[...].astype(BF16)
        k_scr[past:past + seq, :] = k.astype(BF16)
        v_scr[past:past + seq, :] = v.astype(BF16)
    else:
        ko_ref[...] = k
        vo_ref[...] = v
        k_scr[...] = k.astype(BF16)
        v_scr[...] = v.astype(BF16)
    q_scr[...] = (q * (ATT_HD ** -0.5)).astype(BF16)

    grp = ATT_HEADS // ATT_KV_HEADS

    def body(qi, carry):
        r0 = pl.multiple_of(qi * ATT_QB, ATT_QB)
        outs = []
        for h in range(ATT_HEADS):
            g = h // grp
            qh = q_scr[pl.ds(r0, ATT_QB), h * ATT_HD:(h + 1) * ATT_HD]
            kh = k_scr[:, g * ATT_HD:(g + 1) * ATT_HD]
            vh = v_scr[:, g * ATT_HD:(g + 1) * ATT_HD]
            s = lax.dot_general(qh, kh, (((1,), (1,)), ((), ())), preferred_element_type=F32)
            s = s - jnp.max(s, axis=-1, keepdims=True)
            e = jnp.exp(s)
            prob = e / jnp.sum(e, axis=-1, keepdims=True)
            outs.append(jnp.dot(prob.astype(BF16), vh, preferred_element_type=F32))
        o_ref[pl.ds(r0, ATT_QB), :] = jnp.concatenate(outs, axis=1).astype(BF16)
        return carry

    lax.fori_loop(0, seq // ATT_QB, body, 0)


def _attention(p, gq, gk, bdq, bdk, batch, seq, cache=None):
    n = batch * seq
    qw = ATT_HEADS * ATT_HD
    kvw = ATT_KV_HEADS * ATT_HD
    decode = cache is not None
    past = cache[0].shape[1] if decode else 0
    kern = functools.partial(_attention_kernel, seq=seq, decode=decode, past=past)
    in_specs = [
        pl.BlockSpec((seq, qw), lambda b: (b, T_AQ * COL_TILE // qw)),
        pl.BlockSpec((seq, 2 * kvw), lambda b: (b, T_AKV * COL_TILE // (2 * kvw))),
        pl.BlockSpec((1, qw), lambda b: (0, 0)),
        pl.BlockSpec((1, kvw), lambda b: (0, 0)),
        pl.BlockSpec((qw, qw), lambda b: (0, 0)),
        pl.BlockSpec((kvw, kvw), lambda b: (0, 0)),
    ]
    args = [p, p, gq, gk, bdq, bdk]
    out_specs = [pl.BlockSpec((seq, qw), lambda b: (b, 0))]
    out_shape = [jax.ShapeDtypeStruct((n, qw), BF16)]
    if decode:
        ck, cv, cos, sin = cache
        in_specs += [
            pl.BlockSpec((None, past, kvw), lambda b: (b, 0, 0)),
            pl.BlockSpec((None, past, kvw), lambda b: (b, 0, 0)),
            pl.BlockSpec((seq, qw), lambda b: (0, 0)),
            pl.BlockSpec((seq, qw), lambda b: (0, 0)),
        ]
        args += [ck, cv, cos, sin]
    else:
        out_specs += [pl.BlockSpec((None, seq, kvw), lambda b: (b, 0, 0))] * 2
        out_shape += [jax.ShapeDtypeStruct((batch, seq, kvw), F32)] * 2
    return pl.pallas_call(
        kern,
        grid=(batch,),
        in_specs=in_specs,
        out_specs=out_specs,
        out_shape=out_shape,
        scratch_shapes=[pltpu.VMEM((seq, qw), BF16), pltpu.VMEM((past + seq, kvw), BF16),
                        pltpu.VMEM((past + seq, kvw), BF16)],
        compiler_params=_cparams(("arbitrary",)),
        name="attention",
    )(*args)


def _rope_tables(seq):
    t = jnp.arange(seq, dtype=jnp.int32)
    row = (t // GRID_W).astype(F32)
    col = (t % GRID_W).astype(F32)
    nf = ATT_HD // 4
    inv = ROPE_THETA ** (-jnp.arange(nf, dtype=F32) / nf)
    a_row = row[:, None] * inv[None, :]
    a_col = col[:, None] * inv[None, :]
    ang = jnp.concatenate([a_row, a_row, a_col, a_col], axis=1)
    sign = jnp.concatenate([-jnp.ones((nf,), F32), jnp.ones((nf,), F32)] * 2)
    cos = jnp.tile(jnp.cos(ang), (1, ATT_HEADS))
    sin = jnp.tile(jnp.sin(ang) * sign[None, :], (1, ATT_HEADS))
    return cos, sin


def _dft_matrices(seq):
    two_l = 2 * seq
    f = jnp.arange(seq, dtype=jnp.int32)
    t = jnp.arange(seq, dtype=jnp.int32)
    m_fwd = (f[:, None] * t[None, :]) % two_l
    ang_fwd = m_fwd.astype(F32) * (math.pi / seq)
    alt_t = jnp.where(t % 2 == 0, 1.0, -1.0).astype(F32)
    fc = jnp.cos(ang_fwd)
    fs = jnp.where(f[:, None] == 0, alt_t[None, :], jnp.sin(ang_fwd))
    n_out = t + seq // 2
    m_inv = (n_out[:, None] * f[None, :]) % two_l
    ang_inv = m_inv.astype(F32) * (math.pi / seq)
    wgt = jnp.where(f == 0, 1.0, 2.0).astype(F32) / two_l
    alt_n = jnp.where(n_out % 2 == 0, 1.0, -1.0).astype(F32)
    gc = jnp.cos(ang_inv) * wgt[None, :]
    gs = jnp.where(f[None, :] == 0, alt_n[:, None] / two_l, jnp.sin(ang_inv) * wgt[None, :])
    out = []
    for m in (fc, fs, gc, gs):
        hi = m.astype(BF16)
        lo = (m - hi.astype(F32)).astype(BF16)
        out += [hi, lo]
    return tuple(out)


def _hyena_feats(seq):
    t = jnp.arange(seq, dtype=F32) / seq
    bands = jnp.arange(1, HY_BANDS + 1, dtype=F32)
    ang = 2.0 * jnp.pi * t[:, None] * bands[None, :]
    z = jnp.concatenate([t[:, None], jnp.sin(ang), jnp.cos(ang)], -1)
    return jnp.pad(z, ((0, 0), (0, LANES - HY_FEAT)))


def _hyena_filter_kernel(z_ref, w1_ref, b1_ref, w2_ref, b2_ref, fr_ref, w3_ref, lr_ref,
                         fch_ref, fcl_ref, fsh_ref, fsl_ref, hc_ref, hs_ref, *, seq):
    fr = fr_ref[...]
    h = jnp.sin(fr[0:1, :] * (_dot3(z_ref[...], w1_ref[...]) + b1_ref[...]))
    h = jnp.sin(fr[1:2, :] * (_dot3(h, w2_ref[...]) + b2_ref[...]))
    h = _dot3(h, w3_ref[...])
    t = lax.broadcasted_iota(jnp.int32, (seq, 1), 0).astype(F32)
    dist = jnp.abs(t - float(seq // 2)) / (0.5 * seq)
    h = h * jnp.exp(-jnp.exp(lr_ref[...]) * dist)
    hc_ref[...] = _dot3_pre(fch_ref[...], fcl_ref[...], h)
    hs_ref[...] = _dot3_pre(fsh_ref[...], fsl_ref[...], h)


def _hyena_filters(z, w1p, b1, w2, b2, fr, w3, lr, dft, seq):
    fch, fcl, fsh, fsl = dft[:4]
    full = lambda a: pl.BlockSpec(a.shape, lambda: (0,) * a.ndim)
    args = [z, w1p, b1, w2, b2, fr, w3, lr, fch, fcl, fsh, fsl]
    return pl.pallas_call(
        functools.partial(_hyena_filter_kernel, seq=seq),
        in_specs=[full(a) for a in args],
        out_specs=[pl.BlockSpec((seq, 2 * HY_C), lambda: (0, 0))] * 2,
        out_shape=[jax.ShapeDtypeStruct((seq, 2 * HY_C), F32)] * 2,
        compiler_params=pltpu.CompilerParams(vmem_limit_bytes=VMEM_LIMIT),
        name="hyena_filters",
    )(*args)


def _shift_rows(u, seq):
    row = lax.broadcasted_iota(jnp.int32, u.shape, 0)
    prev = jnp.where(row == 0, 0.0, pltpu.roll(u, 1, axis=0))
    nxt = jnp.where(row == seq - 1, 0.0, pltpu.roll(u, seq - 1, axis=0))
    return prev, nxt


def _hyena_kernel(v_ref, x1_ref, x2_ref, cv_ref, c1_ref, c2_ref, bias_ref,
                  hc0_ref, hs0_ref, hc1_ref, hs1_ref,
                  fc_ref, fs_ref, gc_ref, gs_ref, o_ref, *, seq):
    def dwconv(u_ref, w_ref):
        u = u_ref[...]
        w = w_ref[...]
        prev, nxt = _shift_rows(u, seq)
        return prev * w[0:1, :] + u * w[1:2, :] + nxt * w[2:3, :]

    row0 = lax.broadcasted_iota(jnp.int32, (seq, COL_TILE), 0) == 0

    def long_conv(u, hc, hs):
        u16 = u.astype(BF16)
        uc = jnp.dot(fc_ref[...], u16, preferred_element_type=F32)
        us = jnp.dot(fs_ref[...], u16, preferred_element_type=F32)
        ss = us * hs
        yc = uc * hc - jnp.where(row0, 0.0, ss)
        ys = jnp.where(row0, ss, uc * hs + us * hc)
        return (jnp.dot(gc_ref[...], yc.astype(BF16), preferred_element_type=F32)
                + jnp.dot(gs_ref[...], ys.astype(BF16), preferred_element_type=F32))

    bias = bias_ref[...]
    v = dwconv(v_ref, cv_ref)
    z = long_conv(v, hc0_ref[...], hs0_ref[...]) + bias[0:1, :] * v
    z = dwconv(x1_ref, c1_ref) * z
    z = long_conv(z, hc1_ref[...], hs1_ref[...]) + bias[1:2, :] * z
    o_ref[...] = (dwconv(x2_ref, c2_ref) * z).astype(BF16)


def _hyena(p, hy_conv, hy_bias, hc, hs, dft, layer, batch, seq):
    n = batch * seq
    nb = HY_C // COL_TILE
    tile = lambda off: pl.BlockSpec((seq, COL_TILE), lambda b, c: (b, T_HY + off * nb + c))
    cw = lambda off: pl.BlockSpec((None, 3, COL_TILE), lambda b, c: (layer, 0, off * nb + c))
    filt = lambda o: pl.BlockSpec((seq, COL_TILE), lambda b, c: (0, o * nb + c))
    mat = pl.BlockSpec((seq, seq), lambda b, c: (0, 0))
    return pl.pallas_call(
        functools.partial(_hyena_kernel, seq=seq),
        grid=(batch, nb),
        in_specs=[tile(0), tile(1), tile(2), cw(0), cw(1), cw(2),
                  pl.BlockSpec((None, 2, COL_TILE), lambda b, c: (layer, 0, c)),
                  filt(0), filt(0), filt(1), filt(1)] + [mat] * 4,
        out_specs=pl.BlockSpec((seq, COL_TILE), lambda b, c: (b, c)),
        out_shape=jax.ShapeDtypeStruct((n, HY_C), BF16),
        compiler_params=_cparams(("arbitrary", "arbitrary")),
        name="hyena",
    )(p, p, p, hy_conv, hy_conv, hy_conv, hy_bias, hc, hs, hc, hs, dft[0], dft[2], dft[4], dft[6])


def _stack2(x):
    lane = lax.broadcasted_iota(jnp.int32, x.shape, 1)
    first = lane < RW_N
    return jnp.concatenate([jnp.where(first, x, 0.0), jnp.where(first, 0.0, x)], axis=0)


def _rwkv_kernel(*refs, seq, has_s0):
    (pr_ref, pk_ref, pv_ref, pl_ref, mur_ref, muk_ref, muv_ref, mul_ref, aup_ref, gup_ref, wup_ref, w0_ref,
     a0_ref, kk_ref, ka_ref, rk_ref, bd_ref) = refs[:17]
    pos = 17
    s0_ref = None
    if has_s0:
        s0_ref = refs[pos]
        pos += 1
    o_ref, st_ref = refs[pos], refs[pos + 1]
    r_s, k_s, v_s, a_s, b_s, gate_s, bonus_s, y_scr, e_s, pre16, pre32 = refs[pos + 2:pos + 13]

    def shifted(ref, mu_ref):
        p = ref[...]
        prev, nxt = _shift_rows(p, seq)
        return p + mu_ref[...] * (0.5 * (prev + nxt) - p)

    bd = bd_ref[...]
    r = shifted(pr_ref, mur_ref)
    k = shifted(pk_ref, muk_ref)
    v = shifted(pv_ref, muv_ref)
    low = shifted(pl_ref, mul_ref)
    wa = low[:, :LANES]
    iclr = jax.nn.sigmoid(a0_ref[...] + _dot(wa, aup_ref[...]))
    gate_s[...] = _dot(jax.nn.sigmoid(low[:, LANES:]), gup_ref[...])
    kk = k * kk_ref[...]
    kk = kk * lax.rsqrt(_dot_x2(kk * kk, bd) + 1e-12)
    k2 = k * (1.0 + (iclr - 1.0) * ka_ref[...])
    bonus_s[...] = _dot_x2(r * k2 * rk_ref[...], bd) * v
    r_s[...] = r
    k_s[...] = k2
    v_s[...] = v
    a_s[...] = -kk
    b_s[...] = kk * iclr
    tw = jnp.tanh(wa)
    for d in range(2):
        x = -(w0_ref[d:d + 1, :] + _dot(tw, wup_ref[d]))
        softplus = jnp.maximum(x, 0.0) + jnp.log1p(jnp.exp(-jnp.abs(x)))
        e_s[d] = jnp.exp(-softplus - 0.5)

    C = RW_CHUNK
    S = 2 * C
    nc = seq // C
    ri = lax.broadcasted_iota(jnp.int32, (S, S), 0)
    ci = lax.broadcasted_iota(jnp.int32, (S, S), 1)
    eye = ri == ci
    eye_f = jnp.where(eye, 1.0, 0.0)
    tr = ri & (C - 1)
    tc = ci & (C - 1)
    ti = lax.broadcasted_iota(jnp.int32, (C, C), 0)
    tj = lax.broadcasted_iota(jnp.int32, (C, C), 1)
    dir_consts = (
        (tc < tr, tc <= tr, (tj <= ti).astype(BF16), C - 1),
        (tc > tr, tc >= tr, (tj >= ti).astype(BF16), 0),
    )
    level_masks = []
    m = 1
    while m < C:
        lg = m.bit_length() - 1
        same = jnp.right_shift(tr, lg + 1) == jnp.right_shift(tc, lg + 1)
        halves = (jnp.right_shift(tr, lg) & 1) != (jnp.right_shift(tc, lg) & 1)
        level_masks.append(same & halves)
        m *= 2

    def bf(x):
        return x.astype(BF16)

    def mm(x, y):
        return jnp.dot(x, y, preferred_element_type=F32)

    def mm_nt(x, y):
        return lax.dot_general(x, y, (((1,), (1,)), ((), ())), preferred_element_type=F32)

    def mm_tn(x, y):
        return lax.dot_general(x, y, (((0,), (0,)), ((), ())), preferred_element_type=F32)

    def phase1(units):
        fr = []
        for d, c in units:
            strict, incl, tri, last = dir_consts[d]
            rows = pl.ds(pl.multiple_of(c * C, C), C)
            ec = e_s[d, rows, :]
            cum = _dot_m2x(tri, ec)
            g_c = cum[last:last + 1, :]
            inv = jnp.exp(cum)
            to_end = jnp.exp(cum - g_c)
            a16 = bf(_stack2(a_s[rows, :] * jnp.exp(ec - cum)))
            r_t = _stack2(r_s[rows, :] * jnp.exp(-cum))
            bc = b_s[rows, :]
            kc = k_s[rows, :]
            gm = mm_nt(jnp.concatenate([a16, bf(r_t)], axis=0),
                       jnp.concatenate([bf(_stack2(bc * inv)), bf(_stack2(kc * inv))], axis=0))
            fr.append(dict(
                d=d, c=c, g_c=g_c, a16=a16, r_t=r_t,
                bh16=bf(_stack2(bc * to_end)), kh16=bf(_stack2(kc * to_end)), v16=bf(_stack2(v_s[rows, :])),
                n_m=jnp.where(strict, gm[:S, :S], 0.0),
                ak16=bf(jnp.where(strict, gm[:S, S:], 0.0)),
                rb16=bf(jnp.where(incl, gm[S:, :S], 0.0)),
                rk16=bf(jnp.where(incl, gm[S:, S:], 0.0))))
        ts = [None] * len(fr)
        for li, mask in enumerate(level_masks):
            for u, f in enumerate(fr):
                n_off = jnp.where(mask, f['n_m'], 0.0)
                if li == 0:
                    ts[u] = eye_f + n_off
                else:
                    t16 = bf(ts[u])
                    ts[u] = ts[u] + mm(t16, bf(mm(bf(n_off), t16)))
        for u, f in enumerate(fr):
            d, c = f['d'], f['c']
            akv = mm(f['ak16'], f['v16'])
            wu = mm(bf(ts[u]), jnp.concatenate([f['a16'], bf(akv)], axis=1))
            w16 = bf(wu[:, :S])
            uv16 = jnp.concatenate([bf(wu[:, S:]), f['v16']], axis=0)
            pre16[d, c, 0] = bf(f['r_t'] + mm(f['rb16'], w16))
            pre32[d, c, 0] = mm(jnp.concatenate([f['rb16'], f['rk16']], axis=1), uv16)
            pre16[d, c, 1] = bf(jnp.where(eye, jnp.exp(-f['g_c']), 0.0) + mm_tn(w16, f['bh16']))
            pre32[d, c, 1] = mm_tn(uv16, jnp.concatenate([f['bh16'], f['kh16']], axis=0))

    group = min(nc, 4)
    if nc == group:
        phase1([(d, c) for d in range(2) for c in range(nc)])
    else:
        def phase1_body(i, carry):
            phase1([(d, i * group + j) for d in range(2) for j in range(group)])
            return carry

        lax.fori_loop(0, nc // group, phase1_body, 0)

    for d in range(2):
        st_ref[d] = s0_ref[d] if has_s0 else jnp.zeros((S, S), F32)
    y_scr[...] = jnp.zeros_like(y_scr)

    def phase2(i):
        for d in range(2):
            c = i if d == 0 else nc - 1 - i
            rows = pl.ds(pl.multiple_of(c * C, C), C)
            s16 = bf(st_ref[d])
            y_st = mm_nt(pre16[d, c, 0], s16) + pre32[d, c, 0]
            y_scr[rows, :] = y_scr[rows, :] + (y_st[:C, :] + y_st[C:, :])
            st_ref[d] = mm(s16, pre16[d, c, 1]) + pre32[d, c, 1]

    if nc == group:
        for i in range(nc):
            phase2(i)
    else:
        def phase2_body(i, carry):
            phase2(i)
            return carry

        lax.fori_loop(0, nc, phase2_body, 0)

    y = y_scr[...]
    ones_bd = bd_ref[...]
    mean = _dot_x2(y, ones_bd) * (1.0 / RW_N)
    yc = y - mean
    var = _dot_x2(yc * yc, ones_bd) * (1.0 / RW_N)
    yn = yc * lax.rsqrt(var + RW_LN_EPS)
    o_ref[...] = ((yn + bonus_s[...]) * gate_s[...]).astype(BF16)


def _rwkv(p, mu, aup, gup, wup, w0, a0, kk, ka, rk, bd2, s0, batch, seq):
    n = batch * seq
    npair = RW_HEADS // 2
    has_s0 = s0 is not None
    S = 2 * RW_CHUNK
    base = T_RW * COL_TILE // LANES
    per = RW_C // LANES
    pcol = lambda sec: pl.BlockSpec((seq, LANES), lambda b, h: (b, base + sec * per + h))
    mucol = lambda sec: pl.BlockSpec((1, LANES), lambda b, h: (0, sec * per + h))
    vec = pl.BlockSpec((1, LANES), lambda b, h: (0, h))
    low_idx = 3 * RW_C // COL_TILE
    in_specs = [
        pcol(0), pcol(1), pcol(2),
        pl.BlockSpec((seq, COL_TILE), lambda b, h: (b, T_RW + low_idx)),
        mucol(0), mucol(1), mucol(2),
        pl.BlockSpec((1, COL_TILE), lambda b, h: (0, low_idx)),
        pl.BlockSpec((LANES, LANES), lambda b, h: (0, h)),
        pl.BlockSpec((LANES, LANES), lambda b, h: (0, h)),
        pl.BlockSpec((2, LANES, LANES), lambda b, h: (0, 0, h)),
        pl.BlockSpec((2, LANES), lambda b, h: (0, h)),
        vec, vec, vec, vec,
        pl.BlockSpec((LANES, LANES), lambda b, h: (0, 0)),
    ]
    args = [p, p, p, p, mu, mu, mu, mu, aup, gup, wup, w0, a0, kk, ka, rk, bd2]
    if has_s0:
        in_specs.append(pl.BlockSpec((None, 2, None, S, S), lambda b, h: (b, 0, h, 0, 0)))
        args.append(s0)
    seq_buf = pltpu.VMEM((seq, LANES), F32)
    return pl.pallas_call(
        functools.partial(_rwkv_kernel, seq=seq, has_s0=has_s0),
        grid=(batch, npair),
        in_specs=in_specs,
        out_specs=[pl.BlockSpec((seq, LANES), lambda b, h: (b, h)),
                   pl.BlockSpec((None, 2, None, S, S), lambda b, h: (b, 0, h, 0, 0))],
        out_shape=[jax.ShapeDtypeStruct((n, RW_C), BF16), jax.ShapeDtypeStruct((batch, 2, npair, S, S), F32)],
        scratch_shapes=[seq_buf] * 8 + [pltpu.VMEM((2, seq, LANES), F32),
                                        pltpu.VMEM((2, seq // RW_CHUNK, 2, S, S), BF16),
                                        pltpu.VMEM((2, seq // RW_CHUNK, 2, S, S), F32)],
        compiler_params=_cparams(("arbitrary", "arbitrary")),
        name="rwkv7",
    )(*args)


def _pair_states(s):
    b = s.shape[0]
    s = s.reshape(b, 2, RW_HEADS // 2, 2, RW_N, RW_N)
    z = jnp.zeros_like(s[:, :, :, 0])
    top = jnp.concatenate([s[:, :, :, 0], z], axis=-1)
    bot = jnp.concatenate([z, s[:, :, :, 1]], axis=-1)
    return jnp.concatenate([top, bot], axis=-2)


def _unpair_states(s):
    b = s.shape[0]
    h0 = s[:, :, :, :RW_N, :RW_N]
    h1 = s[:, :, :, RW_N:, RW_N:]
    return jnp.stack([h0, h1], axis=3).reshape(b, 2, RW_HEADS, RW_N, RW_N)


MRG_TM = 1024
MRG_TN = 256


def _merge_kernel(h_ref, ba_ref, bb_ref, bc_ref, bd_ref, g0_ref, g1_ref, g2_ref, g3_ref, wb_ref, o_ref):
    h = h_ref[...]
    acc = None
    for i, (br, gw) in enumerate(((ba_ref, g0_ref), (bb_ref, g1_ref), (bc_ref, g2_ref), (bd_ref, g3_ref))):
        gate = jax.nn.sigmoid(jnp.dot(h, gw[...].astype(BF16), preferred_element_type=F32))
        proj = jnp.dot(br[...], wb_ref[i].astype(BF16), preferred_element_type=F32)
        acc = gate * proj if acc is None else acc + gate * proj
    o_ref[...] = acc.astype(BF16)


def _merge(h, branches, w_in, w_branch, layer):
    n = h.shape[0]
    gate_spec = lambda b: pl.BlockSpec(
        (None, D_MODEL, MRG_TN), lambda i, j: (layer, 0, (MIX_W + b * D_MODEL) // MRG_TN + j))
    return pl.pallas_call(
        _merge_kernel,
        grid=(n // MRG_TM, D_MODEL // MRG_TN),
        in_specs=[pl.BlockSpec((MRG_TM, D_MODEL), lambda i, j: (i, 0))]
        + [pl.BlockSpec((MRG_TM, BRANCH_W), lambda i, j: (i, 0))] * N_BRANCH
        + [gate_spec(b) for b in range(N_BRANCH)]
        + [pl.BlockSpec((None, N_BRANCH, BRANCH_W, MRG_TN), lambda i, j: (layer, 0, 0, j))],
        out_specs=pl.BlockSpec((MRG_TM, MRG_TN), lambda i, j: (i, j)),
        out_shape=jax.ShapeDtypeStruct((n, D_MODEL), BF16),
        compiler_params=_cparams(("arbitrary", "arbitrary")),
        name="gated_merge",
    )(h, *branches, w_in, w_in, w_in, w_in, w_branch)


OUT_TM = 256


def _route(logits):
    lane_i = lax.broadcasted_iota(jnp.int32, logits.shape, 1)
    lane = lane_i.astype(F32)
    grp_of_lane = jnp.right_shift(lane_i, MOE_PER_GROUP.bit_length() - 1).astype(F32)
    neg = -jnp.inf
    is_grp = (lane_i >= MOE_EXPERTS) & (lane_i < MOE_EXPERTS + MOE_GROUPS)
    gl = jnp.where(is_grp, logits, neg)
    gmax = jnp.max(gl, axis=-1, keepdims=True)
    gsel = jnp.min(jnp.where(gl == gmax, lane, float(LANES)), axis=-1, keepdims=True) - float(MOE_EXPERTS)
    gval = 1.0 / jnp.sum(jnp.exp(gl - gmax), axis=-1, keepdims=True)
    in_grp = (grp_of_lane == gsel) & (lane_i < MOE_EXPERTS)
    el = jnp.where(in_grp, logits, neg)
    ee = jnp.exp(el - jnp.max(el, axis=-1, keepdims=True))
    prob = jnp.where(in_grp, ee / jnp.sum(ee, axis=-1, keepdims=True), -1.0)
    v1 = jnp.max(prob, axis=-1, keepdims=True)
    i1 = jnp.min(jnp.where(prob == v1, lane, float(LANES)), axis=-1, keepdims=True)
    prob2 = jnp.where(lane == i1, -1.0, prob)
    v2 = jnp.max(prob2, axis=-1, keepdims=True)
    i2 = jnp.min(jnp.where(prob2 == v2, lane, float(LANES)), axis=-1, keepdims=True)
    tot = v1 + v2
    return jnp.where(lane == i1, gval * (v1 / tot), 0.0) + jnp.where(lane == i2, gval * (v2 / tot), 0.0)


def _outproj_kernel(m_ref, w_ref, x_ref, mod_ref, lng_ref, lnb_ref, rwh_ref, rwl_ref, rb_ref,
                    x1_ref, h2_ref, comb_ref, *, req_base, rows_per_req):
    i = pl.program_id(0)
    mix = jnp.dot(m_ref[...], w_ref[...], preferred_element_type=F32)
    m = req_base + (i * OUT_TM) // rows_per_req
    g1 = mod_ref[2, pl.ds(m, 1), :]
    sh2 = mod_ref[3, pl.ds(m, 1), :]
    sc2 = mod_ref[4, pl.ds(m, 1), :]
    x1 = _ln(ALPHA * x_ref[...] + g1 * mix) * lng_ref[0:1, :] + lnb_ref[0:1, :]
    x1_ref[...] = x1
    h2 = _ln(x1) * (1.0 + sc2) + sh2
    h2_ref[...] = h2.astype(BF16)
    hh, hl = _split(h2)
    rwh = rwh_ref[...]
    logits = (jnp.dot(hh, rwh, preferred_element_type=F32) + jnp.dot(hl, rwh, preferred_element_type=F32)
              + jnp.dot(hh, rwl_ref[...], preferred_element_type=F32)) + rb_ref[...]
    comb_ref[...] = _route(logits)


def _outproj(merged, w_out16, x2d, mod_l, ln_g, ln_b, rwh, rwl, rb, layer, req_base, rows_per_req):
    n = x2d.shape[0]
    kern = functools.partial(_outproj_kernel, req_base=req_base, rows_per_req=rows_per_req)
    row = lambda w: pl.BlockSpec((OUT_TM, w), lambda i: (i, 0))
    full = lambda a: pl.BlockSpec(a.shape, lambda i: (0,) * a.ndim)
    lnspec = pl.BlockSpec((None, 2, D_MODEL), lambda i: (layer, 0, 0))
    return pl.pallas_call(
        kern,
        grid=(n // OUT_TM,),
        in_specs=[row(D_MODEL),
                  pl.BlockSpec((None, D_MODEL, D_MODEL), lambda i: (layer, 0, 0)),
                  row(D_MODEL), full(mod_l), lnspec, lnspec, full(rwh), full(rwl), full(rb)],
        out_specs=[row(D_MODEL), row(D_MODEL), row(LANES)],
        out_shape=[jax.ShapeDtypeStruct((n, D_MODEL), F32), jax.ShapeDtypeStruct((n, D_MODEL), BF16),
                   jax.ShapeDtypeStruct((n, LANES), F32)],
        compiler_params=_cparams(("arbitrary",)),
        name="out_projection",
    )(merged, w_out16, x2d, mod_l, ln_g, ln_b, rwh, rwl, rb)


MOE_TM = 1024


def _moe_kernel(h_ref, comb_ref, wg_ref, wu_ref, wd_ref, o_ref):
    e = pl.program_id(1)

    @pl.when(e == 0)
    def _():
        o_ref[...] = jnp.zeros_like(o_ref)

    h = h_ref[...]
    comb = comb_ref[...]
    lane = lax.broadcasted_iota(jnp.int32, comb.shape, 1)
    ce = jnp.sum(jnp.where(lane == e, comb, 0.0), axis=-1, keepdims=True)
    hg = jnp.dot(h, wg_ref[...].astype(BF16), preferred_element_type=F32)
    hu = jnp.dot(h, wu_ref[...].astype(BF16), preferred_element_type=F32)
    act = _silu(hg) * hu * ce
    o_ref[...] += jnp.dot(act.astype(BF16), wd_ref[...].astype(BF16), preferred_element_type=F32)


def _moe(h2, comb, w_gate, w_up, w_down, layer):
    n = h2.shape[0]
    return pl.pallas_call(
        _moe_kernel,
        grid=(n // MOE_TM, MOE_EXPERTS),
        in_specs=[pl.BlockSpec((MOE_TM, D_MODEL), lambda i, e: (i, 0)),
                  pl.BlockSpec((MOE_TM, LANES), lambda i, e: (i, 0)),
                  pl.BlockSpec((None, None, D_MODEL, MOE_HID), lambda i, e: (layer, e, 0, 0)),
                  pl.BlockSpec((None, None, D_MODEL, MOE_HID), lambda i, e: (layer, e, 0, 0)),
                  pl.BlockSpec((None, None, MOE_HID, D_MODEL), lambda i, e: (layer, e, 0, 0))],
        out_specs=pl.BlockSpec((MOE_TM, D_MODEL), lambda i, e: (i, 0)),
        out_shape=jax.ShapeDtypeStruct((n, D_MODEL), F32),
        compiler_params=_cparams(("arbitrary", "arbitrary")),
        name="moe_experts",
    )(h2, comb, w_gate, w_up, w_down)


FIN_TM = 512


def _final_kernel(x_ref, f_ref, mod_ref, lng_ref, lnb_ref, o_ref, *, req_base, rows_per_req):
    i = pl.program_id(0)
    m = req_base + (i * FIN_TM) // rows_per_req
    g2 = mod_ref[5, pl.ds(m, 1), :]
    o_ref[...] = _ln(ALPHA * x_ref[...] + g2 * f_ref[...]) * lng_ref[1:2, :] + lnb_ref[1:2, :]


def _final(x1, ffn, mod_l, ln_g, ln_b, layer, req_base, rows_per_req):
    n = x1.shape[0]
    row = pl.BlockSpec((FIN_TM, D_MODEL), lambda i: (i, 0))
    lnspec = pl.BlockSpec((None, 2, D_MODEL), lambda i: (layer, 0, 0))
    return pl.pallas_call(
        functools.partial(_final_kernel, req_base=req_base, rows_per_req=rows_per_req),
        grid=(n // FIN_TM,),
        in_specs=[row, row, pl.BlockSpec(mod_l.shape, lambda i: (0, 0, 0)), lnspec, lnspec],
        out_specs=row,
        out_shape=jax.ShapeDtypeStruct((n, D_MODEL), F32),
        compiler_params=_cparams(("arbitrary",)),
        name="final_norm",
    )(x1, ffn, mod_l, ln_g, ln_b)


def _block_diag_ones(width, blk):
    i = jnp.arange(width) // blk
    return (i[:, None] == i[None, :]).astype(BF16)


def _layer(x2d, layer, batch, seq, req_base, rows_per_req, mod_l, wts, consts, ctx):
    decode = ctx is not None
    p, h = _inproj(x2d, mod_l, wts['w_in'], layer, req_base, rows_per_req)

    ret = _retention(p, consts['ret_dec'][layer], ctx['ret'] if decode else None, batch, seq,
                     emit_state=not decode)
    cache = (ctx['k'], ctx['v'], consts['rope_cos'], consts['rope_sin']) if decode else None
    att = _attention(p, consts['gq'][layer], consts['gk'][layer], consts['bd_q'], consts['bd_k'],
                     batch, seq, cache)
    hc, hs = consts['hy_filt'][seq][layer]
    o_c = _hyena(p, wts['hy_conv'], wts['hy_bias'], hc, hs, consts['dft'][seq], layer, batch, seq)
    o_d, rw_state = _rwkv(p, consts['rw_mu'][layer], consts['rw_aup'][layer], wts['rw_g_up'][layer],
                          consts['rw_wup'][layer], wts['rw_w0'][layer], consts['rw_a0'][layer],
                          consts['rw_kk'][layer], consts['rw_ka'][layer], consts['rw_rk'][layer],
                          consts['bd_pair'], ctx['rw'] if decode else None, batch, seq)

    merged = _merge(h, (ret[0], att[0], o_c, o_d), wts['w_in'], wts['w_branch'], layer)
    x1, h2, comb = _outproj(merged, consts['w_out16'], x2d, mod_l, wts['ln_g'], wts['ln_b'],
                            consts['router_hi'][layer], consts['router_lo'][layer], consts['router_b'][layer],
                            layer, req_base, rows_per_req)
    ffn = _moe(h2, comb, wts['moe_w_gate'], wts['moe_w_up'], wts['moe_w_down'], layer)
    x2 = _final(x1, ffn, mod_l, wts['ln_g'], wts['ln_b'], layer, req_base, rows_per_req)
    if decode:
        return x2, None
    return x2, (att[1], att[2], ret[1], rw_state)


def kernel(x_prompt, x_sample, c, cache_attn_k, cache_attn_v, state_ret, state_rwkv, c_ctx, mod_w, mod_b, w_in, ret_decay_exp, attn_q_norm, attn_k_norm, hy_conv, hy_w1, hy_b1, hy_w2, hy_b2, hy_freq, hy_w3, hy_log_rate, hy_bias, rw_mu, rw_w0, rw_w_up, rw_a0, rw_a_up, rw_g_up, rw_k_k, rw_k_a, rw_r_k, w_branch, w_out, ln_g, ln_b, moe_rg_w, moe_rg_b, moe_re_w, moe_re_b, moe_w_gate, moe_w_up, moe_w_down):
    batch, seq, _ = x_prompt.shape
    dbatch, dseq, _ = x_sample.shape
    past = cache_attn_k.shape[2]
    kvw = ATT_KV_HEADS * ATT_HD

    cvec = jnp.concatenate([c_ctx[None, :], c, jnp.zeros((8 - 1 - dbatch, D_MODEL), F32)], axis=0)
    mod = _modulation(cvec, mod_w, mod_b)

    wts = dict(w_in=w_in, hy_conv=hy_conv, hy_bias=hy_bias, rw_g_up=rw_g_up, rw_w0=rw_w0, w_branch=w_branch,
               ln_g=ln_g, ln_b=ln_b, moe_w_gate=moe_w_gate, moe_w_up=moe_w_up,
               moe_w_down=moe_w_down)

    zlo = jnp.zeros((DEPTH, RW_N, RW_C), F32)
    router_pad = LANES - MOE_EXPERTS - MOE_GROUPS
    router_w = jnp.pad(jnp.concatenate([moe_re_w, moe_rg_w], axis=2), ((0, 0), (0, 0), (0, router_pad)))
    router_hi = router_w.astype(BF16)
    router_lo = (router_w - router_hi.astype(F32)).astype(BF16)
    rope_cos, rope_sin = _rope_tables(dseq)
    consts = dict(
        ret_dec=jnp.repeat(ret_decay_exp, RET_DV, axis=-1),
        gq=jnp.tile(attn_q_norm, (1, ATT_HEADS))[:, None, :],
        gk=jnp.tile(attn_k_norm, (1, ATT_KV_HEADS))[:, None, :],
        bd_q=_block_diag_ones(ATT_HEADS * ATT_HD, ATT_HD),
        bd_k=_block_diag_ones(kvw, ATT_HD),
        bd_pair=_block_diag_ones(LANES, RW_N),
        rope_cos=rope_cos, rope_sin=rope_sin,
        rw_mu=rw_mu[:, None, :],
        rw_aup=jnp.concatenate([zlo, rw_a_up], axis=1),
        rw_wup=jnp.concatenate([rw_w_up, jnp.zeros_like(rw_w_up)], axis=2),
        rw_a0=rw_a0[:, None, :], rw_kk=rw_k_k[:, None, :], rw_ka=rw_k_a[:, None, :],
        rw_rk=rw_r_k.reshape(DEPTH, 1, RW_C),
        w_out16=w_out.astype(BF16),
        router_hi=router_hi, router_lo=router_lo,
        router_b=jnp.pad(jnp.concatenate([moe_re_b, moe_rg_b], axis=1), ((0, 0), (0, router_pad)))[:, None, :],
        dft={}, hy_filt={},
    )
    w1p = jnp.pad(hy_w1, ((0, 0), (0, LANES - HY_FEAT), (0, 0)))
    for s in sorted({seq, dseq}):
        dft = _dft_matrices(s)
        z = _hyena_feats(s)
        consts['dft'][s] = dft
        consts['hy_filt'][s] = [
            _hyena_filters(z, w1p[l], hy_b1[l][None, :], hy_w2[l], hy_b2[l][None, :], hy_freq[l], hy_w3[l],
                           hy_log_rate[l].reshape(1, 2 * HY_C), dft, s)
            for l in range(DEPTH)]

    y = x_prompt.reshape(batch * seq, D_MODEL)
    ks, vs, rets, rws = [], [], [], []
    for l in range(DEPTH):
        y, (k_l, v_l, ret_l, rw_l) = _layer(y, l, batch, seq, 0, batch * seq, mod[l], wts, consts, None)
        ks.append(k_l)
        vs.append(v_l)
        rets.append(ret_l)
        rws.append(_unpair_states(rw_l))
    y_prompt = y.reshape(batch, seq, D_MODEL)
    new_k = jnp.stack(ks, 1).reshape(batch, DEPTH, seq, ATT_KV_HEADS, ATT_HD)
    new_v = jnp.stack(vs, 1).reshape(batch, DEPTH, seq, ATT_KV_HEADS, ATT_HD)
    new_ret = jnp.stack(rets, 1)
    new_rw = jnp.stack(rws, 1)

    ys = x_sample.reshape(dbatch * dseq, D_MODEL)
    for l in range(DEPTH):
        ctx = dict(k=cache_attn_k[:, l].reshape(dbatch, past, kvw), v=cache_attn_v[:, l].reshape(dbatch, past, kvw),
                   ret=state_ret[:, l], rw=_pair_states(state_rwkv[:, l]))
        ys, _ = _layer(ys, l, dbatch, dseq, 1, dseq, mod[l], wts, consts, ctx)
    y_sample = ys.reshape(dbatch, dseq, D_MODEL)

    return (y_prompt, y_sample, new_k, new_v, new_ret, new_rw)
```

```python
import functools
import math

import jax
import jax.numpy as jnp
from jax import lax
from jax.experimental import pallas as pl
from jax.experimental.pallas import tpu as pltpu

F32 = jnp.float32
BF16 = jnp.bfloat16

D_MODEL = 2048
DEPTH = 4
GRID_W = 64
RET_CHUNK = 128
EPS = 1e-5
RW_LN_EPS = 64e-5
RET_HEADS = 4
RET_DK = 128
RET_DV = 128
ATT_HEADS = 8
ATT_KV_HEADS = 2
ATT_HD = 64
ROPE_THETA = 10000.0
HY_C = 512
HY_BANDS = 8
HY_FEAT = 1 + 2 * HY_BANDS
HY_HID = 64
RW_HEADS = 8
RW_N = 64
RW_C = RW_HEADS * RW_N
RW_CHUNK = 64
N_BRANCH = 4
BRANCH_W = 512
MOE_GROUPS = 4
MOE_PER_GROUP = 8
MOE_EXPERTS = MOE_GROUPS * MOE_PER_GROUP
MOE_HID = 256
N_MOD = 6
ALPHA = (2.0 * DEPTH) ** 0.25

MIX_W = 6144
COL_TILE = 256
LANES = 128
VMEM_LIMIT = 56 * 1024 * 1024

T_RQ, T_RK, T_RV, T_RG = 0, 2, 4, 6
T_AQ, T_AKV = 8, 10
T_HY = 11
T_RW = 17


def _cparams(sem):
    return pltpu.CompilerParams(dimension_semantics=sem, vmem_limit_bytes=VMEM_LIMIT)


def _dot(a, b):
    return jnp.dot(a.astype(BF16), b.astype(BF16), preferred_element_type=F32)


def _dot_nt(a, b):
    return lax.dot_general(a.astype(BF16), b.astype(BF16), (((1,), (1,)), ((), ())), preferred_element_type=F32)


def _dot_tn(a, b):
    return lax.dot_general(a.astype(BF16), b.astype(BF16), (((0,), (0,)), ((), ())), preferred_element_type=F32)


def _split(x):
    hi = x.astype(BF16)
    lo = (x - hi.astype(F32)).astype(BF16)
    return hi, lo


def _dot_x2(x, m):
    hi, lo = _split(x)
    return jnp.dot(hi, m, preferred_element_type=F32) + jnp.dot(lo, m, preferred_element_type=F32)


def _dot_m2x(m, x):
    hi, lo = _split(x)
    return jnp.dot(m, hi, preferred_element_type=F32) + jnp.dot(m, lo, preferred_element_type=F32)


def _dot3(x, y):
    xh, xl = _split(x)
    yh, yl = _split(y)
    return (jnp.dot(xh, yh, preferred_element_type=F32) + jnp.dot(xl, yh, preferred_element_type=F32)
            + jnp.dot(xh, yl, preferred_element_type=F32))


def _dot3_pre(mh, ml, x):
    xh, xl = _split(x)
    return (jnp.dot(mh, xh, preferred_element_type=F32) + jnp.dot(mh, xl, preferred_element_type=F32)
            + jnp.dot(ml, xh, preferred_element_type=F32))


def _ln(x, eps=EPS):
    mu = jnp.mean(x, axis=-1, keepdims=True)
    xc = x - mu
    var = jnp.mean(xc * xc, axis=-1, keepdims=True)
    return xc * lax.rsqrt(var + eps)


def _silu(x):
    return x * jax.nn.sigmoid(x)


MOD_TN = 1024


def _mod_kernel(c_ref, w_ref, b_ref, o_ref):
    s = _silu(c_ref[...])
    o_ref[...] = _dot(s, w_ref[...]) + b_ref[...]


def _modulation(cvec, mod_w, mod_b):
    per = D_MODEL // MOD_TN
    return pl.pallas_call(
        _mod_kernel,
        grid=(DEPTH, N_MOD * per),
        in_specs=[
            pl.BlockSpec((8, D_MODEL), lambda l, n: (0, 0)),
            pl.BlockSpec((None, D_MODEL, MOD_TN), lambda l, n: (l, 0, n)),
            pl.BlockSpec((None, 1, MOD_TN), lambda l, n: (l, 0, n)),
        ],
        out_specs=pl.BlockSpec((None, None, 8, MOD_TN), lambda l, n: (l, n // per, 0, n % per)),
        out_shape=jax.ShapeDtypeStruct((DEPTH, N_MOD, 8, D_MODEL), F32),
        compiler_params=_cparams(("arbitrary", "arbitrary")),
        name="modulation",
    )(cvec, mod_w, mod_b.reshape(DEPTH, 1, N_MOD * D_MODEL))


INP_TM = 1024
INP_TN = 512


def _inproj_kernel(x_ref, mod_ref, w_ref, p_ref, h_ref, *, req_base, rows_per_req):
    i = pl.program_id(0)
    j = pl.program_id(1)

    @pl.when(j == 0)
    def _():
        m = req_base + (i * INP_TM) // rows_per_req
        sh = mod_ref[0, pl.ds(m, 1), :]
        sc = mod_ref[1, pl.ds(m, 1), :]
        h_ref[...] = (_ln(x_ref[...]) * (1.0 + sc) + sh).astype(BF16)

    p_ref[...] = jnp.dot(h_ref[...], w_ref[...].astype(BF16), preferred_element_type=F32)


def _inproj(x2d, mod_l, w_in, layer, req_base, rows_per_req):
    n = x2d.shape[0]
    kern = functools.partial(_inproj_kernel, req_base=req_base, rows_per_req=rows_per_req)
    return pl.pallas_call(
        kern,
        grid=(n // INP_TM, MIX_W // INP_TN),
        in_specs=[
            pl.BlockSpec((INP_TM, D_MODEL), lambda i, j: (i, 0)),
            pl.BlockSpec((N_MOD, 8, D_MODEL), lambda i, j: (0, 0, 0)),
            pl.BlockSpec((None, D_MODEL, INP_TN), lambda i, j: (layer, 0, j)),
        ],
        out_specs=[
            pl.BlockSpec((INP_TM, INP_TN), lambda i, j: (i, j)),
            pl.BlockSpec((INP_TM, D_MODEL), lambda i, j: (i, 0)),
        ],
        out_shape=[jax.ShapeDtypeStruct((n, MIX_W), F32), jax.ShapeDtypeStruct((n, D_MODEL), BF16)],
        compiler_params=_cparams(("arbitrary", "arbitrary")),
        name="in_projection",
    )(x2d, mod_l, w_in)


def _retention_kernel(*refs, seq, has_s0, emit_state):
    q_ref, k_ref, v_ref, g_ref, dec_ref = refs[:5]
    pos = 5
    s0_ref = None
    if has_s0:
        s0_ref = refs[pos]
        pos += 1
    o_ref = refs[pos]
    pos += 1
    st_ref = None
    if emit_state:
        st_ref = refs[pos]
        pos += 1
    s_scr, o_scr = refs[pos], refs[pos + 1]

    C = RET_CHUNK
    nc = seq // C
    lg_all = jnp.log1p(-jnp.exp2(-dec_ref[...]))
    ii = lax.broadcasted_iota(jnp.int32, (C, C), 0)
    jj = lax.broadcasted_iota(jnp.int32, (C, C), 1)
    rel = (ii - jj).astype(F32)
    icol = lax.broadcasted_iota(jnp.int32, (C, 1), 0).astype(F32)

    for d in range(2):
        for h in range(RET_HEADS):
            if has_s0:
                s_scr[d, h] = s0_ref[d, h]
            else:
                s_scr[d, h] = jnp.zeros((RET_DK, RET_DV), F32)

    consts = {}
    for d in range(2):
        for h in range(RET_HEADS):
            lg = lg_all[d:d + 1, h * RET_DV:(h + 1) * RET_DV]
            lg1 = lg[:, :1]
            if d == 0:
                dmask = jnp.where(rel >= 0, jnp.exp(lg * rel), 0.0)
                q_dec = jnp.exp(lg1 * (icol + 1.0))
                k_dec = jnp.exp(lg1 * (C - 1.0 - icol))
            else:
                dmask = jnp.where(rel <= 0, jnp.exp(lg * (-rel)), 0.0)
                q_dec = jnp.exp(lg1 * (C - icol))
                k_dec = jnp.exp(lg1 * icol)
            consts[d, h] = (dmask, q_dec, k_dec, jnp.exp(lg * float(C)))

    o_scr[...] = jnp.zeros_like(o_scr)

    def body(ci, carry):
        units = []
        for d in range(2):
            c = ci if d == 0 else nc - 1 - ci
            rows = pl.ds(pl.multiple_of(c * C, C), C)
            for h in range(RET_HEADS):
                cs = slice(h * RET_DK, (h + 1) * RET_DK)
                kc = k_ref[rows, cs] * (RET_DK ** -0.5)
                units.append(dict(d=d, h=h, rows=rows, cs=cs, q16=q_ref[rows, cs].astype(BF16), kc=kc,
                                  v16=v_ref[rows, cs].astype(BF16), s=s_scr[d, h]))
        scores = [_dot_nt(u['q16'], u['kc']) for u in units]
        cross = [_dot(u['q16'], u['s']) for u in units]
        kv = [_dot_tn(u['kc'] * consts[u['d'], u['h']][2], u['v16']) for u in units]
        inner = [_dot(sc * consts[u['d'], u['h']][0], u['v16']) for sc, u in zip(scores, units)]
        for u, cr, upd, inn in zip(units, cross, kv, inner):
            _, q_dec, _, c_dec = consts[u['d'], u['h']]
            s_scr[u['d'], u['h']] = u['s'] * c_dec + upd
            o_scr[u['rows'], u['cs']] = o_scr[u['rows'], u['cs']] + (inn + cr * q_dec)
        return carry

    lax.fori_loop(0, nc, body, 0)

    for h in range(RET_HEADS):
        cs = slice(h * RET_DV, (h + 1) * RET_DV)
        o_ref[:, cs] = (_ln(o_scr[:, cs]) * _silu(g_ref[:, cs])).astype(BF16)
    if emit_state:
        st_ref[...] = s_scr[...]


def _retention(p, dec_rep, s0, batch, seq, emit_state):
    n = batch * seq
    w = RET_HEADS * RET_DK
    has_s0 = s0 is not None
    kern = functools.partial(_retention_kernel, seq=seq, has_s0=has_s0, emit_state=emit_state)
    in_specs = [
        pl.BlockSpec((seq, w), lambda b: (b, 0)),
        pl.BlockSpec((seq, w), lambda b: (b, 1)),
        pl.BlockSpec((seq, w), lambda b: (b, 2)),
        pl.BlockSpec((seq, w), lambda b: (b, 3)),
        pl.BlockSpec((2, w), lambda b: (0, 0)),
    ]
    args = [p, p, p, p, dec_rep]
    if has_s0:
        in_specs.append(pl.BlockSpec((None, 2, RET_HEADS, RET_DK, RET_DV), lambda b: (b, 0, 0, 0, 0)))
        args.append(s0)
    out_specs = [pl.BlockSpec((seq, w), lambda b: (b, 0))]
    out_shape = [jax.ShapeDtypeStruct((n, w), BF16)]
    if emit_state:
        out_specs.append(pl.BlockSpec((None, 2, RET_HEADS, RET_DK, RET_DV), lambda b: (b, 0, 0, 0, 0)))
        out_shape.append(jax.ShapeDtypeStruct((batch, 2, RET_HEADS, RET_DK, RET_DV), F32))
    return pl.pallas_call(
        kern,
        grid=(batch,),
        in_specs=in_specs,
        out_specs=out_specs,
        out_shape=out_shape,
        scratch_shapes=[pltpu.VMEM((2, RET_HEADS, RET_DK, RET_DV), F32), pltpu.VMEM((seq, w), F32)],
        compiler_params=_cparams(("arbitrary",)),
        name="retention",
    )(*args)


ATT_QB = 256


def _head_rms(x, ones_bd, gain):
    ss = _dot_x2(x * x, ones_bd)
    return x * lax.rsqrt(ss * (1.0 / ATT_HD) + EPS) * gain


def _rope(x, cos, sin_signed):
    w = x.shape[-1]
    lane = lax.broadcasted_iota(jnp.int32, x.shape, 1)
    first = (lane & 16) == 0
    swapped = jnp.where(first, pltpu.roll(x, w - 16, axis=1), pltpu.roll(x, 16, axis=1))
    return x * cos + swapped * sin_signed


def _attention_kernel(*refs, seq, decode, past):
    q_ref, kv_ref, gq_ref, gk_ref, bdq_ref, bdk_ref = refs[:6]
    pos = 6
    if decode:
        ck_ref, cv_ref, cos_ref, sin_ref = refs[pos:pos + 4]
        pos += 4
    o_ref = refs[pos]
    pos += 1
    if not decode:
        ko_ref, vo_ref = refs[pos:pos + 2]
        pos += 2
    q_scr, k_scr, v_scr = refs[pos:pos + 3]

    kvw = ATT_KV_HEADS * ATT_HD
    q = _head_rms(q_ref[...], bdq_ref[...], gq_ref[...])
    kv = kv_ref[...]
    k = _head_rms(kv[:, :kvw], bdk_ref[...], gk_ref[...])
    v = kv[:, kvw:]
    if decode:
        cos = cos_ref[...]
        sin = sin_ref[...]
        q = _rope(q, cos, sin)
        k = _rope(k, cos[:, :kvw], sin[:, :kvw])
        k_scr[0:past, :] = ck_ref[...].astype(BF16)
        v_scr[0:past, :] = cv_ref[...].astype(BF16)
        k_scr[past:past + seq, :] = k.astype(BF16)
        v_scr[past:past + seq, :] = v.astype(BF16)
    else:
        ko_ref[...] = k
        vo_ref[...] = v
        k_scr[...] = k.astype(BF16)
        v_scr[...] = v.astype(BF16)
    q_scr[...] = (q * (ATT_HD ** -0.5)).astype(BF16)

    grp = ATT_HEADS // ATT_KV_HEADS

    def body(qi, carry):
        r0 = pl.multiple_of(qi * ATT_QB, ATT_QB)
        outs = []
        for h in range(ATT_HEADS):
            g = h // grp
            qh = q_scr[pl.ds(r0, ATT_QB), h * ATT_HD:(h + 1) * ATT_HD]
            kh = k_scr[:, g * ATT_HD:(g + 1) * ATT_HD]
            vh = v_scr[:, g * ATT_HD:(g + 1) * ATT_HD]
            s = lax.dot_general(qh, kh, (((1,), (1,)), ((), ())), preferred_element_type=F32)
            s = s - jnp.max(s, axis=-1, keepdims=True)
            e = jnp.exp(s)
            prob = e / jnp.sum(e, axis=-1, keepdims=True)
            outs.append(jnp.dot(prob.astype(BF16), vh, preferred_element_type=F32))
        o_ref[pl.ds(r0, ATT_QB), :] = jnp.concatenate(outs, axis=1).astype(BF16)
        return carry

    lax.fori_loop(0, seq // ATT_QB, body, 0)


def _attention(p, gq, gk, bdq, bdk, batch, seq, cache=None):
    n = batch * seq
    qw = ATT_HEADS * ATT_HD
    kvw = ATT_KV_HEADS * ATT_HD
    decode = cache is not None
    past = cache[0].shape[1] if decode else 0
    kern = functools.partial(_attention_kernel, seq=seq, decode=decode, past=past)
    in_specs = [
        pl.BlockSpec((seq, qw), lambda b: (b, T_AQ * COL_TILE // qw)),
        pl.BlockSpec((seq, 2 * kvw), lambda b: (b, T_AKV * COL_TILE // (2 * kvw))),
        pl.BlockSpec((1, qw), lambda b: (0, 0)),
        pl.BlockSpec((1, kvw), lambda b: (0, 0)),
        pl.BlockSpec((qw, qw), lambda b: (0, 0)),
        pl.BlockSpec((kvw, kvw), lambda b: (0, 0)),
    ]
    args = [p, p, gq, gk, bdq, bdk]
    out_specs = [pl.BlockSpec((seq, qw), lambda b: (b, 0))]
    out_shape = [jax.ShapeDtypeStruct((n, qw), BF16)]
    if decode:
        ck, cv, cos, sin = cache
        in_specs += [
            pl.BlockSpec((None, past, kvw), lambda b: (b, 0, 0)),
            pl.BlockSpec((None, past, kvw), lambda b: (b, 0, 0)),
            pl.BlockSpec((seq, qw), lambda b: (0, 0)),
            pl.BlockSpec((seq, qw), lambda b: (0, 0)),
        ]
        args += [ck, cv, cos, sin]
    else:
        out_specs += [pl.BlockSpec((None, seq, kvw), lambda b: (b, 0, 0))] * 2
        out_shape += [jax.ShapeDtypeStruct((batch, seq, kvw), F32)] * 2
    return pl.pallas_call(
        kern,
        grid=(batch,),
        in_specs=in_specs,
        out_specs=out_specs,
        out_shape=out_shape,
        scratch_shapes=[pltpu.VMEM((seq, qw), BF16), pltpu.VMEM((past + seq, kvw), BF16),
                        pltpu.VMEM((past + seq, kvw), BF16)],
        compiler_params=_cparams(("arbitrary",)),
        name="attention",
    )(*args)


def _rope_tables(seq):
    t = jnp.arange(seq, dtype=jnp.int32)
    row = (t // GRID_W).astype(F32)
    col = (t % GRID_W).astype(F32)
    nf = ATT_HD // 4
    inv = ROPE_THETA ** (-jnp.arange(nf, dtype=F32) / nf)
    a_row = row[:, None] * inv[None, :]
    a_col = col[:, None] * inv[None, :]
    ang = jnp.concatenate([a_row, a_row, a_col, a_col], axis=1)
    sign = jnp.concatenate([-jnp.ones((nf,), F32), jnp.ones((nf,), F32)] * 2)
    cos = jnp.tile(jnp.cos(ang), (1, ATT_HEADS))
    sin = jnp.tile(jnp.sin(ang) * sign[None, :], (1, ATT_HEADS))
    return cos, sin


def _dft_matrices(seq):
    two_l = 2 * seq
    f = jnp.arange(seq, dtype=jnp.int32)
    t = jnp.arange(seq, dtype=jnp.int32)
    m_fwd = (f[:, None] * t[None, :]) % two_l
    ang_fwd = m_fwd.astype(F32) * (math.pi / seq)
    alt_t = jnp.where(t % 2 == 0, 1.0, -1.0).astype(F32)
    fc = jnp.cos(ang_fwd)
    fs = jnp.where(f[:, None] == 0, alt_t[None, :], jnp.sin(ang_fwd))
    n_out = t + seq // 2
    m_inv = (n_out[:, None] * f[None, :]) % two_l
    ang_inv = m_inv.astype(F32) * (math.pi / seq)
    wgt = jnp.where(f == 0, 1.0, 2.0).astype(F32) / two_l
    alt_n = jnp.where(n_out % 2 == 0, 1.0, -1.0).astype(F32)
    gc = jnp.cos(ang_inv) * wgt[None, :]
    gs = jnp.where(f[None, :] == 0, alt_n[:, None] / two_l, jnp.sin(ang_inv) * wgt[None, :])
    out = []
    for m in (fc, fs, gc, gs):
        hi = m.astype(BF16)
        lo = (m - hi.astype(F32)).astype(BF16)
        out += [hi, lo]
    return tuple(out)


def _hyena_feats(seq):
    t = jnp.arange(seq, dtype=F32) / seq
    bands = jnp.arange(1, HY_BANDS + 1, dtype=F32)
    ang = 2.0 * jnp.pi * t[:, None] * bands[None, :]
    z = jnp.concatenate([t[:, None], jnp.sin(ang), jnp.cos(ang)], -1)
    return jnp.pad(z, ((0, 0), (0, LANES - HY_FEAT)))


def _hyena_filter_kernel(z_ref, w1_ref, b1_ref, w2_ref, b2_ref, fr_ref, w3_ref, lr_ref,
                         fch_ref, fcl_ref, fsh_ref, fsl_ref, hc_ref, hs_ref, *, seq):
    fr = fr_ref[...]
    h = jnp.sin(fr[0:1, :] * (_dot3(z_ref[...], w1_ref[...]) + b1_ref[...]))
    h = jnp.sin(fr[1:2, :] * (_dot3(h, w2_ref[...]) + b2_ref[...]))
    h = _dot3(h, w3_ref[...])
    t = lax.broadcasted_iota(jnp.int32, (seq, 1), 0).astype(F32)
    dist = jnp.abs(t - float(seq // 2)) / (0.5 * seq)
    h = h * jnp.exp(-jnp.exp(lr_ref[...]) * dist)
    hc_ref[...] = _dot3_pre(fch_ref[...], fcl_ref[...], h)
    hs_ref[...] = _dot3_pre(fsh_ref[...], fsl_ref[...], h)


def _hyena_filters(z, w1p, b1, w2, b2, fr, w3, lr, dft, seq):
    fch, fcl, fsh, fsl = dft[:4]
    full = lambda a: pl.BlockSpec(a.shape, lambda: (0,) * a.ndim)
    args = [z, w1p, b1, w2, b2, fr, w3, lr, fch, fcl, fsh, fsl]
    return pl.pallas_call(
        functools.partial(_hyena_filter_kernel, seq=seq),
        in_specs=[full(a) for a in args],
        out_specs=[pl.BlockSpec((seq, 2 * HY_C), lambda: (0, 0))] * 2,
        out_shape=[jax.ShapeDtypeStruct((seq, 2 * HY_C), F32)] * 2,
        compiler_params=pltpu.CompilerParams(vmem_limit_bytes=VMEM_LIMIT),
        name="hyena_filters",
    )(*args)


def _shift_rows(u, seq):
    row = lax.broadcasted_iota(jnp.int32, u.shape, 0)
    prev = jnp.where(row == 0, 0.0, pltpu.roll(u, 1, axis=0))
    nxt = jnp.where(row == seq - 1, 0.0, pltpu.roll(u, seq - 1, axis=0))
    return prev, nxt


def _hyena_kernel(v_ref, x1_ref, x2_ref, cv_ref, c1_ref, c2_ref, bias_ref,
                  hc0_ref, hs0_ref, hc1_ref, hs1_ref,
                  fc_ref, fs_ref, gc_ref, gs_ref, o_ref, *, seq):
    def dwconv(u_ref, w_ref):
        u = u_ref[...]
        w = w_ref[...]
        prev, nxt = _shift_rows(u, seq)
        return prev * w[0:1, :] + u * w[1:2, :] + nxt * w[2:3, :]

    row0 = lax.broadcasted_iota(jnp.int32, (seq, COL_TILE), 0) == 0

    def long_conv(u, hc, hs):
        u16 = u.astype(BF16)
        uc = jnp.dot(fc_ref[...], u16, preferred_element_type=F32)
        us = jnp.dot(fs_ref[...], u16, preferred_element_type=F32)
        ss = us * hs
        yc = uc * hc - jnp.where(row0, 0.0, ss)
        ys = jnp.where(row0, ss, uc * hs + us * hc)
        return (jnp.dot(gc_ref[...], yc.astype(BF16), preferred_element_type=F32)
                + jnp.dot(gs_ref[...], ys.astype(BF16), preferred_element_type=F32))

    bias = bias_ref[...]
    v = dwconv(v_ref, cv_ref)
    z = long_conv(v, hc0_ref[...], hs0_ref[...]) + bias[0:1, :] * v
    z = dwconv(x1_ref, c1_ref) * z
    z = long_conv(z, hc1_ref[...], hs1_ref[...]) + bias[1:2, :] * z
    o_ref[...] = (dwconv(x2_ref, c2_ref) * z).astype(BF16)


def _hyena(p, hy_conv, hy_bias, hc, hs, dft, layer, batch, seq):
    n = batch * seq
    nb = HY_C // COL_TILE
    tile = lambda off: pl.BlockSpec((seq, COL_TILE), lambda b, c: (b, T_HY + off * nb + c))
    cw = lambda off: pl.BlockSpec((None, 3, COL_TILE), lambda b, c: (layer, 0, off * nb + c))
    filt = lambda o: pl.BlockSpec((seq, COL_TILE), lambda b, c: (0, o * nb + c))
    mat = pl.BlockSpec((seq, seq), lambda b, c: (0, 0))
    return pl.pallas_call(
        functools.partial(_hyena_kernel, seq=seq),
        grid=(batch, nb),
        in_specs=[tile(0), tile(1), tile(2), cw(0), cw(1), cw(2),
                  pl.BlockSpec((None, 2, COL_TILE), lambda b, c: (layer, 0, c)),
                  filt(0), filt(0), filt(1), filt(1)] + [mat] * 4,
        out_specs=pl.BlockSpec((seq, COL_TILE), lambda b, c: (b, c)),
        out_shape=jax.ShapeDtypeStruct((n, HY_C), BF16),
        compiler_params=_cparams(("arbitrary", "arbitrary")),
        name="hyena",
    )(p, p, p, hy_conv, hy_conv, hy_conv, hy_bias, hc, hs, hc, hs, dft[0], dft[2], dft[4], dft[6])


def _stack2(x):
    lane = lax.broadcasted_iota(jnp.int32, x.shape, 1)
    first = lane < RW_N
    return jnp.concatenate([jnp.where(first, x, 0.0), jnp.where(first, 0.0, x)], axis=0)


def _rwkv_kernel(*refs, seq, has_s0):
    (pr_ref, pk_ref, pv_ref, pl_ref, mur_ref, muk_ref, muv_ref, mul_ref, aup_ref, gup_ref, wup_ref, w0_ref,
     a0_ref, kk_ref, ka_ref, rk_ref, bd_ref) = refs[:17]
    pos = 17
    s0_ref = None
    if has_s0:
        s0_ref = refs[pos]
        pos += 1
    o_ref, st_ref = refs[pos], refs[pos + 1]
    r_s, k_s, v_s, a_s, b_s, gate_s, bonus_s, y_scr, e_s, pre16, pre32 = refs[pos + 2:pos + 13]

    def shifted(ref, mu_ref):
        p = ref[...]
        prev, nxt = _shift_rows(p, seq)
        return p + mu_ref[...] * (0.5 * (prev + nxt) - p)

    bd = bd_ref[...]
    r = shifted(pr_ref, mur_ref)
    k = shifted(pk_ref, muk_ref)
    v = shifted(pv_ref, muv_ref)
    low = shifted(pl_ref, mul_ref)
    wa = low[:, :LANES]
    iclr = jax.nn.sigmoid(a0_ref[...] + _dot(wa, aup_ref[...]))
    gate_s[...] = _dot(jax.nn.sigmoid(low[:, LANES:]), gup_ref[...])
    kk = k * kk_ref[...]
    kk = kk * lax.rsqrt(_dot_x2(kk * kk, bd) + 1e-12)
    k2 = k * (1.0 + (iclr - 1.0) * ka_ref[...])
    bonus_s[...] = _dot_x2(r * k2 * rk_ref[...], bd) * v
    r_s[...] = r
    k_s[...] = k2
    v_s[...] = v
    a_s[...] = -kk
    b_s[...] = kk * iclr
    tw = jnp.tanh(wa)
    for d in range(2):
        x = -(w0_ref[d:d + 1, :] + _dot(tw, wup_ref[d]))
        softplus = jnp.maximum(x, 0.0) + jnp.log1p(jnp.exp(-jnp.abs(x)))
        e_s[d] = jnp.exp(-softplus - 0.5)

    C = RW_CHUNK
    S = 2 * C
    nc = seq // C
    ri = lax.broadcasted_iota(jnp.int32, (S, S), 0)
    ci = lax.broadcasted_iota(jnp.int32, (S, S), 1)
    eye = ri == ci
    eye_f = jnp.where(eye, 1.0, 0.0)
    tr = ri & (C - 1)
    tc = ci & (C - 1)
    ti = lax.broadcasted_iota(jnp.int32, (C, C), 0)
    tj = lax.broadcasted_iota(jnp.int32, (C, C), 1)
    dir_consts = (
        (tc < tr, tc <= tr, (tj <= ti).astype(BF16), C - 1),
        (tc > tr, tc >= tr, (tj >= ti).astype(BF16), 0),
    )
    level_masks = []
    m = 1
    while m < C:
        lg = m.bit_length() - 1
        same = jnp.right_shift(tr, lg + 1) == jnp.right_shift(tc, lg + 1)
        halves = (jnp.right_shift(tr, lg) & 1) != (jnp.right_shift(tc, lg) & 1)
        level_masks.append(same & halves)
        m *= 2

    def bf(x):
        return x.astype(BF16)

    def mm(x, y):
        return jnp.dot(x, y, preferred_element_type=F32)

    def mm_nt(x, y):
        return lax.dot_general(x, y, (((1,), (1,)), ((), ())), preferred_element_type=F32)

    def mm_tn(x, y):
        return lax.dot_general(x, y, (((0,), (0,)), ((), ())), preferred_element_type=F32)

    def phase1(units):
        fr = []
        row_sl = [pl.ds(pl.multiple_of(c * C, C), C) for _, c in units]
        ecs = [e_s[d, rows, :] for (d, _), rows in zip(units, row_sl)]
        cums = [_dot_m2x(dir_consts[d][2], ec) for (d, _), ec in zip(units, ecs)]
        for (d, c), rows, ec, cum in zip(units, row_sl, ecs, cums):
            strict, incl, tri, last = dir_consts[d]
            g_c = cum[last:last + 1, :]
            inv = jnp.exp(cum)
            to_end = jnp.exp(cum - g_c)
            a16 = bf(_stack2(a_s[rows, :] * jnp.exp(ec - cum)))
            r_t = _stack2(r_s[rows, :] * jnp.exp(-cum))
            bc = b_s[rows, :]
            kc = k_s[rows, :]
            gm = mm_nt(jnp.concatenate([a16, bf(r_t)], axis=0),
                       jnp.concatenate([bf(_stack2(bc * inv)), bf(_stack2(kc * inv))], axis=0))
            fr.append(dict(
                d=d, c=c, g_c=g_c, a16=a16, r_t=r_t,
                bh16=bf(_stack2(bc * to_end)), kh16=bf(_stack2(kc * to_end)), v16=bf(_stack2(v_s[rows, :])),
                n_m=jnp.where(strict, gm[:S, :S], 0.0),
                ak16=bf(jnp.where(strict, gm[:S, S:], 0.0)),
                rb16=bf(jnp.where(incl, gm[S:, :S], 0.0)),
                rk16=bf(jnp.where(incl, gm[S:, S:], 0.0))))
        nu = len(fr)
        ts = [None] * nu
        for li, mask in enumerate(level_masks):
            if li == 0:
                ts = [eye_f + jnp.where(mask, f['n_m'], 0.0) for f in fr]
                continue
            t16 = [bf(t) for t in ts]
            inner = [mm(bf(jnp.where(mask, fr[u]['n_m'], 0.0)), t16[u]) for u in range(nu)]
            ts = [ts[u] + mm(t16[u], bf(inner[u])) for u in range(nu)]
        akv = [mm(f['ak16'], f['v16']) for f in fr]
        wu = [mm(bf(ts[u]), jnp.concatenate([fr[u]['a16'], bf(akv[u])], axis=1)) for u in range(nu)]
        w16 = [bf(x[:, :S]) for x in wu]
        uv16 = [jnp.concatenate([bf(wu[u][:, S:]), fr[u]['v16']], axis=0) for u in range(nu)]
        for u, f in enumerate(fr):
            d, c = f['d'], f['c']
            pre16[d, c, 0] = bf(f['r_t'] + mm(f['rb16'], w16[u]))
            pre32[d, c, 0] = mm(jnp.concatenate([f['rb16'], f['rk16']], axis=1), uv16[u])
            pre16[d, c, 1] = bf(jnp.where(eye, jnp.exp(-f['g_c']), 0.0) + mm_tn(w16[u], f['bh16']))
            pre32[d, c, 1] = mm_tn(uv16[u], jnp.concatenate([f['bh16'], f['kh16']], axis=0))

    group = min(nc, 4)
    if nc == group:
        phase1([(d, c) for d in range(2) for c in range(nc)])
    else:
        def phase1_body(i, carry):
            phase1([(d, i * group + j) for d in range(2) for j in range(group)])
            return carry

        lax.fori_loop(0, nc // group, phase1_body, 0)

    for d in range(2):
        st_ref[d] = s0_ref[d] if has_s0 else jnp.zeros((S, S), F32)
    y_scr[...] = jnp.zeros_like(y_scr)

    def phase2(i):
        cs = (i, nc - 1 - i)
        s16 = [bf(st_ref[d]) for d in range(2)]
        for d in range(2):
            st_ref[d] = mm(s16[d], pre16[d, cs[d], 1]) + pre32[d, cs[d], 1]
        for d in range(2):
            rows = pl.ds(pl.multiple_of(cs[d] * C, C), C)
            y_st = mm_nt(pre16[d, cs[d], 0], s16[d]) + pre32[d, cs[d], 0]
            y_scr[rows, :] = y_scr[rows, :] + (y_st[:C, :] + y_st[C:, :])

    if nc == group:
        for i in range(nc):
            phase2(i)
    else:
        def phase2_body(i, carry):
            phase2(i)
            return carry

        lax.fori_loop(0, nc, phase2_body, 0)

    y = y_scr[...]
    ones_bd = bd_ref[...]
    mean = _dot_x2(y, ones_bd) * (1.0 / RW_N)
    yc = y - mean
    var = _dot_x2(yc * yc, ones_bd) * (1.0 / RW_N)
    yn = yc * lax.rsqrt(var + RW_LN_EPS)
    o_ref[...] = ((yn + bonus_s[...]) * gate_s[...]).astype(BF16)


def _rwkv(p, mu, aup, gup, wup, w0, a0, kk, ka, rk, bd2, s0, batch, seq):
    n = batch * seq
    npair = RW_HEADS // 2
    has_s0 = s0 is not None
    S = 2 * RW_CHUNK
    base = T_RW * COL_TILE // LANES
    per = RW_C // LANES
    pcol = lambda sec: pl.BlockSpec((seq, LANES), lambda b, h: (b, base + sec * per + h))
    mucol = lambda sec: pl.BlockSpec((1, LANES), lambda b, h: (0, sec * per + h))
    vec = pl.BlockSpec((1, LANES), lambda b, h: (0, h))
    low_idx = 3 * RW_C // COL_TILE
    in_specs = [
        pcol(0), pcol(1), pcol(2),
        pl.BlockSpec((seq, COL_TILE), lambda b, h: (b, T_RW + low_idx)),
        mucol(0), mucol(1), mucol(2),
        pl.BlockSpec((1, COL_TILE), lambda b, h: (0, low_idx)),
        pl.BlockSpec((LANES, LANES), lambda b, h: (0, h)),
        pl.BlockSpec((LANES, LANES), lambda b, h: (0, h)),
        pl.BlockSpec((2, LANES, LANES), lambda b, h: (0, 0, h)),
        pl.BlockSpec((2, LANES), lambda b, h: (0, h)),
        vec, vec, vec, vec,
        pl.BlockSpec((LANES, LANES), lambda b, h: (0, 0)),
    ]
    args = [p, p, p, p, mu, mu, mu, mu, aup, gup, wup, w0, a0, kk, ka, rk, bd2]
    if has_s0:
        in_specs.append(pl.BlockSpec((None, 2, None, S, S), lambda b, h: (b, 0, h, 0, 0)))
        args.append(s0)
    seq_buf = pltpu.VMEM((seq, LANES), F32)
    return pl.pallas_call(
        functools.partial(_rwkv_kernel, seq=seq, has_s0=has_s0),
        grid=(batch, npair),
        in_specs=in_specs,
        out_specs=[pl.BlockSpec((seq, LANES), lambda b, h: (b, h)),
                   pl.BlockSpec((None, 2, None, S, S), lambda b, h: (b, 0, h, 0, 0))],
        out_shape=[jax.ShapeDtypeStruct((n, RW_C), BF16), jax.ShapeDtypeStruct((batch, 2, npair, S, S), F32)],
        scratch_shapes=[seq_buf] * 8 + [pltpu.VMEM((2, seq, LANES), F32),
                                        pltpu.VMEM((2, seq // RW_CHUNK, 2, S, S), BF16),
                                        pltpu.VMEM((2, seq // RW_CHUNK, 2, S, S), F32)],
        compiler_params=_cparams(("arbitrary", "arbitrary")),
        name="rwkv7",
    )(*args)


def _pair_states(s):
    b = s.shape[0]
    s = s.reshape(b, 2, RW_HEADS // 2, 2, RW_N, RW_N)
    z = jnp.zeros_like(s[:, :, :, 0])
    top = jnp.concatenate([s[:, :, :, 0], z], axis=-1)
    bot = jnp.concatenate([z, s[:, :, :, 1]], axis=-1)
    return jnp.concatenate([top, bot], axis=-2)


def _unpair_states(s):
    b = s.shape[0]
    h0 = s[:, :, :, :RW_N, :RW_N]
    h1 = s[:, :, :, RW_N:, RW_N:]
    return jnp.stack([h0, h1], axis=3).reshape(b, 2, RW_HEADS, RW_N, RW_N)


MRG_TM = 1024
MRG_TN = 256


def _merge_kernel(h_ref, ba_ref, bb_ref, bc_ref, bd_ref, g0_ref, g1_ref, g2_ref, g3_ref, wb_ref, o_ref):
    h = h_ref[...]
    acc = None
    for i, (br, gw) in enumerate(((ba_ref, g0_ref), (bb_ref, g1_ref), (bc_ref, g2_ref), (bd_ref, g3_ref))):
        gate = jax.nn.sigmoid(jnp.dot(h, gw[...].astype(BF16), preferred_element_type=F32))
        proj = jnp.dot(br[...], wb_ref[i].astype(BF16), preferred_element_type=F32)
        acc = gate * proj if acc is None else acc + gate * proj
    o_ref[...] = acc.astype(BF16)


def _merge(h, branches, w_in, w_branch, layer):
    n = h.shape[0]
    gate_spec = lambda b: pl.BlockSpec(
        (None, D_MODEL, MRG_TN), lambda i, j: (layer, 0, (MIX_W + b * D_MODEL) // MRG_TN + j))
    return pl.pallas_call(
        _merge_kernel,
        grid=(n // MRG_TM, D_MODEL // MRG_TN),
        in_specs=[pl.BlockSpec((MRG_TM, D_MODEL), lambda i, j: (i, 0))]
        + [pl.BlockSpec((MRG_TM, BRANCH_W), lambda i, j: (i, 0))] * N_BRANCH
        + [gate_spec(b) for b in range(N_BRANCH)]
        + [pl.BlockSpec((None, N_BRANCH, BRANCH_W, MRG_TN), lambda i, j: (layer, 0, 0, j))],
        out_specs=pl.BlockSpec((MRG_TM, MRG_TN), lambda i, j: (i, j)),
        out_shape=jax.ShapeDtypeStruct((n, D_MODEL), BF16),
        compiler_params=_cparams(("arbitrary", "arbitrary")),
        name="gated_merge",
    )(h, *branches, w_in, w_in, w_in, w_in, w_branch)


OUT_TM = 256


def _route(logits):
    lane_i = lax.broadcasted_iota(jnp.int32, logits.shape, 1)
    lane = lane_i.astype(F32)
    grp_of_lane = jnp.right_shift(lane_i, MOE_PER_GROUP.bit_length() - 1).astype(F32)
    neg = -jnp.inf
    is_grp = (lane_i >= MOE_EXPERTS) & (lane_i < MOE_EXPERTS + MOE_GROUPS)
    gl = jnp.where(is_grp, logits, neg)
    gmax = jnp.max(gl, axis=-1, keepdims=True)
    gsel = jnp.min(jnp.where(gl == gmax, lane, float(LANES)), axis=-1, keepdims=True) - float(MOE_EXPERTS)
    gval = 1.0 / jnp.sum(jnp.exp(gl - gmax), axis=-1, keepdims=True)
    in_grp = (grp_of_lane == gsel) & (lane_i < MOE_EXPERTS)
    el = jnp.where(in_grp, logits, neg)
    ee = jnp.exp(el - jnp.max(el, axis=-1, keepdims=True))
    prob = jnp.where(in_grp, ee / jnp.sum(ee, axis=-1, keepdims=True), -1.0)
    v1 = jnp.max(prob, axis=-1, keepdims=True)
    i1 = jnp.min(jnp.where(prob == v1, lane, float(LANES)), axis=-1, keepdims=True)
    prob2 = jnp.where(lane == i1, -1.0, prob)
    v2 = jnp.max(prob2, axis=-1, keepdims=True)
    i2 = jnp.min(jnp.where(prob2 == v2, lane, float(LANES)), axis=-1, keepdims=True)
    tot = v1 + v2
    return jnp.where(lane == i1, gval * (v1 / tot), 0.0) + jnp.where(lane == i2, gval * (v2 / tot), 0.0)


def _outproj_kernel(m_ref, w_ref, x_ref, mod_ref, lng_ref, lnb_ref, rwh_ref, rwl_ref, rb_ref,
                    x1_ref, h2_ref, comb_ref, *, req_base, rows_per_req):
    i = pl.program_id(0)
    mix = jnp.dot(m_ref[...], w_ref[...], preferred_element_type=F32)
    m = req_base + (i * OUT_TM) // rows_per_req
    g1 = mod_ref[2, pl.ds(m, 1), :]
    sh2 = mod_ref[3, pl.ds(m, 1), :]
    sc2 = mod_ref[4, pl.ds(m, 1), :]
    x1 = _ln(ALPHA * x_ref[...] + g1 * mix) * lng_ref[0:1, :] + lnb_ref[0:1, :]
    x1_ref[...] = x1
    h2 = _ln(x1) * (1.0 + sc2) + sh2
    h2_ref[...] = h2.astype(BF16)
    hh, hl = _split(h2)
    rwh = rwh_ref[...]
    logits = (jnp.dot(hh, rwh, preferred_element_type=F32) + jnp.dot(hl, rwh, preferred_element_type=F32)
              + jnp.dot(hh, rwl_ref[...], preferred_element_type=F32)) + rb_ref[...]
    comb_ref[...] = _route(logits)


def _outproj(merged, w_out16, x2d, mod_l, ln_g, ln_b, rwh, rwl, rb, layer, req_base, rows_per_req):
    n = x2d.shape[0]
    kern = functools.partial(_outproj_kernel, req_base=req_base, rows_per_req=rows_per_req)
    row = lambda w: pl.BlockSpec((OUT_TM, w), lambda i: (i, 0))
    full = lambda a: pl.BlockSpec(a.shape, lambda i: (0,) * a.ndim)
    lnspec = pl.BlockSpec((None, 2, D_MODEL), lambda i: (layer, 0, 0))
    return pl.pallas_call(
        kern,
        grid=(n // OUT_TM,),
        in_specs=[row(D_MODEL),
                  pl.BlockSpec((None, D_MODEL, D_MODEL), lambda i: (layer, 0, 0)),
                  row(D_MODEL), full(mod_l), lnspec, lnspec, full(rwh), full(rwl), full(rb)],
        out_specs=[row(D_MODEL), row(D_MODEL), row(LANES)],
        out_shape=[jax.ShapeDtypeStruct((n, D_MODEL), F32), jax.ShapeDtypeStruct((n, D_MODEL), BF16),
                   jax.ShapeDtypeStruct((n, LANES), F32)],
        compiler_params=_cparams(("arbitrary",)),
        name="out_projection",
    )(merged, w_out16, x2d, mod_l, ln_g, ln_b, rwh, rwl, rb)


MOE_TM = 1024


def _moe_kernel(h_ref, comb_ref, wg_ref, wu_ref, wd_ref, o_ref):
    e = pl.program_id(1)

    @pl.when(e == 0)
    def _():
        o_ref[...] = jnp.zeros_like(o_ref)

    h = h_ref[...]
    comb = comb_ref[...]
    lane = lax.broadcasted_iota(jnp.int32, comb.shape, 1)
    ce = jnp.sum(jnp.where(lane == e, comb, 0.0), axis=-1, keepdims=True)
    hg = jnp.dot(h, wg_ref[...].astype(BF16), preferred_element_type=F32)
    hu = jnp.dot(h, wu_ref[...].astype(BF16), preferred_element_type=F32)
    act = _silu(hg) * hu * ce
    o_ref[...] += jnp.dot(act.astype(BF16), wd_ref[...].astype(BF16), preferred_element_type=F32)


def _moe(h2, comb, w_gate, w_up, w_down, layer):
    n = h2.shape[0]
    return pl.pallas_call(
        _moe_kernel,
        grid=(n // MOE_TM, MOE_EXPERTS),
        in_specs=[pl.BlockSpec((MOE_TM, D_MODEL), lambda i, e: (i, 0)),
                  pl.BlockSpec((MOE_TM, LANES), lambda i, e: (i, 0)),
                  pl.BlockSpec((None, None, D_MODEL, MOE_HID), lambda i, e: (layer, e, 0, 0)),
                  pl.BlockSpec((None, None, D_MODEL, MOE_HID), lambda i, e: (layer, e, 0, 0)),
                  pl.BlockSpec((None, None, MOE_HID, D_MODEL), lambda i, e: (layer, e, 0, 0))],
        out_specs=pl.BlockSpec((MOE_TM, D_MODEL), lambda i, e: (i, 0)),
        out_shape=jax.ShapeDtypeStruct((n, D_MODEL), F32),
        compiler_params=_cparams(("arbitrary", "arbitrary")),
        name="moe_experts",
    )(h2, comb, w_gate, w_up, w_down)


FIN_TM = 512


def _final_kernel(x_ref, f_ref, mod_ref, lng_ref, lnb_ref, o_ref, *, req_base, rows_per_req):
    i = pl.program_id(0)
    m = req_base + (i * FIN_TM) // rows_per_req
    g2 = mod_ref[5, pl.ds(m, 1), :]
    o_ref[...] = _ln(ALPHA * x_ref[...] + g2 * f_ref[...]) * lng_ref[1:2, :] + lnb_ref[1:2, :]


def _final(x1, ffn, mod_l, ln_g, ln_b, layer, req_base, rows_per_req):
    n = x1.shape[0]
    row = pl.BlockSpec((FIN_TM, D_MODEL), lambda i: (i, 0))
    lnspec = pl.BlockSpec((None, 2, D_MODEL), lambda i: (layer, 0, 0))
    return pl.pallas_call(
        functools.partial(_final_kernel, req_base=req_base, rows_per_req=rows_per_req),
        grid=(n // FIN_TM,),
        in_specs=[row, row, pl.BlockSpec(mod_l.shape, lambda i: (0, 0, 0)), lnspec, lnspec],
        out_specs=row,
        out_shape=jax.ShapeDtypeStruct((n, D_MODEL), F32),
        compiler_params=_cparams(("arbitrary",)),
        name="final_norm",
    )(x1, ffn, mod_l, ln_g, ln_b)


def _block_diag_ones(width, blk):
    i = jnp.arange(width) // blk
    return (i[:, None] == i[None, :]).astype(BF16)


def _layer(x2d, layer, batch, seq, req_base, rows_per_req, mod_l, wts, consts, ctx):
    decode = ctx is not None
    p, h = _inproj(x2d, mod_l, wts['w_in'], layer, req_base, rows_per_req)

    ret = _retention(p, consts['ret_dec'][layer], ctx['ret'] if decode else None, batch, seq,
                     emit_state=not decode)
    cache = (ctx['k'], ctx['v'], consts['rope_cos'], consts['rope_sin']) if decode else None
    att = _attention(p, consts['gq'][layer], consts['gk'][layer], consts['bd_q'], consts['bd_k'],
                     batch, seq, cache)
    hc, hs = consts['hy_filt'][seq][layer]
    o_c = _hyena(p, wts['hy_conv'], wts['hy_bias'], hc, hs, consts['dft'][seq], layer, batch, seq)
    o_d, rw_state = _rwkv(p, consts['rw_mu'][layer], consts['rw_aup'][layer], wts['rw_g_up'][layer],
                          consts['rw_wup'][layer], wts['rw_w0'][layer], consts['rw_a0'][layer],
                          consts['rw_kk'][layer], consts['rw_ka'][layer], consts['rw_rk'][layer],
                          consts['bd_pair'], ctx['rw'] if decode else None, batch, seq)

    merged = _merge(h, (ret[0], att[0], o_c, o_d), wts['w_in'], wts['w_branch'], layer)
    x1, h2, comb = _outproj(merged, consts['w_out16'], x2d, mod_l, wts['ln_g'], wts['ln_b'],
                            consts['router_hi'][layer], consts['router_lo'][layer], consts['router_b'][layer],
                            layer, req_base, rows_per_req)
    ffn = _moe(h2, comb, wts['moe_w_gate'], wts['moe_w_up'], wts['moe_w_down'], layer)
    x2 = _final(x1, ffn, mod_l, wts['ln_g'], wts['ln_b'], layer, req_base, rows_per_req)
    if decode:
        return x2, None
    return x2, (att[1], att[2], ret[1], rw_state)


def kernel(x_prompt, x_sample, c, cache_attn_k, cache_attn_v, state_ret, state_rwkv, c_ctx, mod_w, mod_b, w_in, ret_decay_exp, attn_q_norm, attn_k_norm, hy_conv, hy_w1, hy_b1, hy_w2, hy_b2, hy_freq, hy_w3, hy_log_rate, hy_bias, rw_mu, rw_w0, rw_w_up, rw_a0, rw_a_up, rw_g_up, rw_k_k, rw_k_a, rw_r_k, w_branch, w_out, ln_g, ln_b, moe_rg_w, moe_rg_b, moe_re_w, moe_re_b, moe_w_gate, moe_w_up, moe_w_down):
    batch, seq, _ = x_prompt.shape
    dbatch, dseq, _ = x_sample.shape
    past = cache_attn_k.shape[2]
    kvw = ATT_KV_HEADS * ATT_HD

    cvec = jnp.concatenate([c_ctx[None, :], c, jnp.zeros((8 - 1 - dbatch, D_MODEL), F32)], axis=0)
    mod = _modulation(cvec, mod_w, mod_b)

    wts = dict(w_in=w_in, hy_conv=hy_conv, hy_bias=hy_bias, rw_g_up=rw_g_up, rw_w0=rw_w0, w_branch=w_branch,
               ln_g=ln_g, ln_b=ln_b, moe_w_gate=moe_w_gate, moe_w_up=moe_w_up,
               moe_w_down=moe_w_down)

    zlo = jnp.zeros((DEPTH, RW_N, RW_C), F32)
    router_pad = LANES - MOE_EXPERTS - MOE_GROUPS
    router_w = jnp.pad(jnp.concatenate([moe_re_w, moe_rg_w], axis=2), ((0, 0), (0, 0), (0, router_pad)))
    router_hi = router_w.astype(BF16)
    router_lo = (router_w - router_hi.astype(F32)).astype(BF16)
    rope_cos, rope_sin = _rope_tables(dseq)
    consts = dict(
        ret_dec=jnp.repeat(ret_decay_exp, RET_DV, axis=-1),
        gq=jnp.tile(attn_q_norm, (1, ATT_HEADS))[:, None, :],
        gk=jnp.tile(attn_k_norm, (1, ATT_KV_HEADS))[:, None, :],
        bd_q=_block_diag_ones(ATT_HEADS * ATT_HD, ATT_HD),
        bd_k=_block_diag_ones(kvw, ATT_HD),
        bd_pair=_block_diag_ones(LANES, RW_N),
        rope_cos=rope_cos, rope_sin=rope_sin,
        rw_mu=rw_mu[:, None, :],
        rw_aup=jnp.concatenate([zlo, rw_a_up], axis=1),
        rw_wup=jnp.concatenate([rw_w_up, jnp.zeros_like(rw_w_up)], axis=2),
        rw_a0=rw_a0[:, None, :], rw_kk=rw_k_k[:, None, :], rw_ka=rw_k_a[:, None, :],
        rw_rk=rw_r_k.reshape(DEPTH, 1, RW_C),
        w_out16=w_out.astype(BF16),
        router_hi=router_hi, router_lo=router_lo,
        router_b=jnp.pad(jnp.concatenate([moe_re_b, moe_rg_b], axis=1), ((0, 0), (0, router_pad)))[:, None, :],
        dft={}, hy_filt={},
    )
    w1p = jnp.pad(hy_w1, ((0, 0), (0, LANES - HY_FEAT), (0, 0)))
    for s in sorted({seq, dseq}):
        dft = _dft_matrices(s)
        z = _hyena_feats(s)
        consts['dft'][s] = dft
        consts['hy_filt'][s] = [
            _hyena_filters(z, w1p[l], hy_b1[l][None, :], hy_w2[l], hy_b2[l][None, :], hy_freq[l], hy_w3[l],
                           hy_log_rate[l].reshape(1, 2 * HY_C), dft, s)
            for l in range(DEPTH)]

    y = x_prompt.reshape(batch * seq, D_MODEL)
    ks, vs, rets, rws = [], [], [], []
    for l in range(DEPTH):
        y, (k_l, v_l, ret_l, rw_l) = _layer(y, l, batch, seq, 0, batch * seq, mod[l], wts, consts, None)
        ks.append(k_l)
        vs.append(v_l)
        rets.append(ret_l)
        rws.append(_unpair_states(rw_l))
    y_prompt = y.reshape(batch, seq, D_MODEL)
    new_k = jnp.stack(ks, 1).reshape(batch, DEPTH, seq, ATT_KV_HEADS, ATT_HD)
    new_v = jnp.stack(vs, 1).reshape(batch, DEPTH, seq, ATT_KV_HEADS, ATT_HD)
    new_ret = jnp.stack(rets, 1)
    new_rw = jnp.stack(rws, 1)

    ys = x_sample.reshape(dbatch * dseq, D_MODEL)
    for l in range(DEPTH):
        ctx = dict(k=cache_attn_k[:, l].reshape(dbatch, past, kvw), v=cache_attn_v[:, l].reshape(dbatch, past, kvw),
                   ret=state_ret[:, l], rw=_pair_states(state_rwkv[:, l]))
        ys, _ = _layer(ys, l, dbatch, dseq, 1, dseq, mod[l], wts, consts, ctx)
    y_sample = ys.reshape(dbatch, dseq, D_MODEL)

    return (y_prompt, y_sample, new_k, new_v, new_ret, new_rw)
```

```python
import functools
import math

import jax
import jax.numpy as jnp
from jax import lax
from jax.experimental import pallas as pl
from jax.experimental.pallas import tpu as pltpu

F32 = jnp.float32
BF16 = jnp.bfloat16

D_MODEL = 2048
DEPTH = 4
GRID_W = 64
RET_CHUNK = 128
EPS = 1e-5
RW_LN_EPS = 64e-5
RET_HEADS = 4
RET_DK = 128
RET_DV = 128
ATT_HEADS = 8
ATT_KV_HEADS = 2
ATT_HD = 64
ROPE_THETA = 10000.0
HY_C = 512
HY_BANDS = 8
HY_FEAT = 1 + 2 * HY_BANDS
HY_HID = 64
RW_HEADS = 8
RW_N = 64
RW_C = RW_HEADS * RW_N
RW_CHUNK = 64
N_BRANCH = 4
BRANCH_W = 512
MOE_GROUPS = 4
MOE_PER_GROUP = 8
MOE_EXPERTS = MOE_GROUPS * MOE_PER_GROUP
MOE_HID = 256
N_MOD = 6
ALPHA = (2.0 * DEPTH) ** 0.25

MIX_W = 6144
COL_TILE = 256
LANES = 128
VMEM_LIMIT = 56 * 1024 * 1024

T_RQ, T_RK, T_RV, T_RG = 0, 2, 4, 6
T_AQ, T_AKV = 8, 10
T_HY = 11
T_RW = 17


def _cparams(sem):
    return pltpu.CompilerParams(dimension_semantics=sem, vmem_limit_bytes=VMEM_LIMIT)


def _dot(a, b):
    return jnp.dot(a.astype(BF16), b.astype(BF16), preferred_element_type=F32)


def _dot_nt(a, b):
    return lax.dot_general(a.astype(BF16), b.astype(BF16), (((1,), (1,)), ((), ())), preferred_element_type=F32)


def _dot_tn(a, b):
    return lax.dot_general(a.astype(BF16), b.astype(BF16), (((0,), (0,)), ((), ())), preferred_element_type=F32)


def _split(x):
    hi = x.astype(BF16)
    lo = (x - hi.astype(F32)).astype(BF16)
    return hi, lo


def _dot_x2(x, m):
    hi, lo = _split(x)
    return jnp.dot(hi, m, preferred_element_type=F32) + jnp.dot(lo, m, preferred_element_type=F32)


def _dot_m2x(m, x):
    hi, lo = _split(x)
    return jnp.dot(m, hi, preferred_element_type=F32) + jnp.dot(m, lo, preferred_element_type=F32)


def _dot3(x, y):
    xh, xl = _split(x)
    yh, yl = _split(y)
    return (jnp.dot(xh, yh, preferred_element_type=F32) + jnp.dot(xl, yh, preferred_element_type=F32)
            + jnp.dot(xh, yl, preferred_element_type=F32))


def _dot3_pre(mh, ml, x):
    xh, xl = _split(x)
    return (jnp.dot(mh, xh, preferred_element_type=F32) + jnp.dot(mh, xl, preferred_element_type=F32)
            + jnp.dot(ml, xh, preferred_element_type=F32))


def _ln(x, eps=EPS):
    mu = jnp.mean(x, axis=-1, keepdims=True)
    xc = x - mu
    var = jnp.mean(xc * xc, axis=-1, keepdims=True)
    return xc * lax.rsqrt(var + eps)


def _silu(x):
    return x * jax.nn.sigmoid(x)


MOD_TN = 1024


def _mod_kernel(c_ref, w_ref, b_ref, o_ref):
    s = _silu(c_ref[...])
    o_ref[...] = _dot(s, w_ref[...]) + b_ref[...]


def _modulation(cvec, mod_w, mod_b):
    per = D_MODEL // MOD_TN
    return pl.pallas_call(
        _mod_kernel,
        grid=(DEPTH, N_MOD * per),
        in_specs=[
            pl.BlockSpec((8, D_MODEL), lambda l, n: (0, 0)),
            pl.BlockSpec((None, D_MODEL, MOD_TN), lambda l, n: (l, 0, n)),
            pl.BlockSpec((None, 1, MOD_TN), lambda l, n: (l, 0, n)),
        ],
        out_specs=pl.BlockSpec((None, None, 8, MOD_TN), lambda l, n: (l, n // per, 0, n % per)),
        out_shape=jax.ShapeDtypeStruct((DEPTH, N_MOD, 8, D_MODEL), F32),
        compiler_params=_cparams(("arbitrary", "arbitrary")),
        name="modulation",
    )(cvec, mod_w, mod_b.reshape(DEPTH, 1, N_MOD * D_MODEL))


INP_TM = 1024
INP_TN = 512


def _inproj_kernel(x_ref, mod_ref, w_ref, p_ref, h_ref, *, req_base, rows_per_req):
    i = pl.program_id(0)
    j = pl.program_id(1)

    @pl.when(j == 0)
    def _():
        m = req_base + (i * INP_TM) // rows_per_req
        sh = mod_ref[0, pl.ds(m, 1), :]
        sc = mod_ref[1, pl.ds(m, 1), :]
        h_ref[...] = (_ln(x_ref[...]) * (1.0 + sc) + sh).astype(BF16)

    p_ref[...] = jnp.dot(h_ref[...], w_ref[...].astype(BF16), preferred_element_type=F32)


def _inproj(x2d, mod_l, w_in, layer, req_base, rows_per_req):
    n = x2d.shape[0]
    kern = functools.partial(_inproj_kernel, req_base=req_base, rows_per_req=rows_per_req)
    return pl.pallas_call(
        kern,
        grid=(n // INP_TM, MIX_W // INP_TN),
        in_specs=[
            pl.BlockSpec((INP_TM, D_MODEL), lambda i, j: (i, 0)),
            pl.BlockSpec((N_MOD, 8, D_MODEL), lambda i, j: (0, 0, 0)),
            pl.BlockSpec((None, D_MODEL, INP_TN), lambda i, j: (layer, 0, j)),
        ],
        out_specs=[
            pl.BlockSpec((INP_TM, INP_TN), lambda i, j: (i, j)),
            pl.BlockSpec((INP_TM, D_MODEL), lambda i, j: (i, 0)),
        ],
        out_shape=[jax.ShapeDtypeStruct((n, MIX_W), F32), jax.ShapeDtypeStruct((n, D_MODEL), BF16)],
        compiler_params=_cparams(("arbitrary", "arbitrary")),
        name="in_projection",
    )(x2d, mod_l, w_in)


def _retention_kernel(*refs, seq, has_s0, emit_state):
    q_ref, k_ref, v_ref, g_ref, dec_ref = refs[:5]
    pos = 5
    s0_ref = None
    if has_s0:
        s0_ref = refs[pos]
        pos += 1
    o_ref = refs[pos]
    pos += 1
    st_ref = None
    if emit_state:
        st_ref = refs[pos]
        pos += 1
    s_scr, o_scr = refs[pos], refs[pos + 1]

    C = RET_CHUNK
    nc = seq // C
    lg_all = jnp.log1p(-jnp.exp2(-dec_ref[...]))
    ii = lax.broadcasted_iota(jnp.int32, (C, C), 0)
    jj = lax.broadcasted_iota(jnp.int32, (C, C), 1)
    rel = (ii - jj).astype(F32)
    icol = lax.broadcasted_iota(jnp.int32, (C, 1), 0).astype(F32)

    for d in range(2):
        for h in range(RET_HEADS):
            if has_s0:
                s_scr[d, h] = s0_ref[d, h]
            else:
                s_scr[d, h] = jnp.zeros((RET_DK, RET_DV), F32)

    consts = {}
    for d in range(2):
        for h in range(RET_HEADS):
            lg = lg_all[d:d + 1, h * RET_DV:(h + 1) * RET_DV]
            lg1 = lg[:, :1]
            if d == 0:
                dmask = jnp.where(rel >= 0, jnp.exp(lg * rel), 0.0)
                q_dec = jnp.exp(lg1 * (icol + 1.0))
                k_dec = jnp.exp(lg1 * (C - 1.0 - icol))
            else:
                dmask = jnp.where(rel <= 0, jnp.exp(lg * (-rel)), 0.0)
                q_dec = jnp.exp(lg1 * (C - icol))
                k_dec = jnp.exp(lg1 * icol)
            consts[d, h] = (dmask, q_dec, k_dec, jnp.exp(lg * float(C)))

    o_scr[...] = jnp.zeros_like(o_scr)

    def body(ci, carry):
        units = []
        for d in range(2):
            c = ci if d == 0 else nc - 1 - ci
            rows = pl.ds(pl.multiple_of(c * C, C), C)
            for h in range(RET_HEADS):
                cs = slice(h * RET_DK, (h + 1) * RET_DK)
                kc = k_ref[rows, cs] * (RET_DK ** -0.5)
                units.append(dict(d=d, h=h, rows=rows, cs=cs, q16=q_ref[rows, cs].astype(BF16), kc=kc,
                                  v16=v_ref[rows, cs].astype(BF16), s=s_scr[d, h]))
        scores = [_dot_nt(u['q16'], u['kc']) for u in units]
        cross = [_dot(u['q16'], u['s']) for u in units]
        kv = [_dot_tn(u['kc'] * consts[u['d'], u['h']][2], u['v16']) for u in units]
        inner = [_dot(sc * consts[u['d'], u['h']][0], u['v16']) for sc, u in zip(scores, units)]
        for u, cr, upd, inn in zip(units, cross, kv, inner):
            _, q_dec, _, c_dec = consts[u['d'], u['h']]
            s_scr[u['d'], u['h']] = u['s'] * c_dec + upd
            o_scr[u['rows'], u['cs']] = o_scr[u['rows'], u['cs']] + (inn + cr * q_dec)
        return carry

    lax.fori_loop(0, nc, body, 0)

    for h in range(RET_HEADS):
        cs = slice(h * RET_DV, (h + 1) * RET_DV)
        o_ref[:, cs] = (_ln(o_scr[:, cs]) * _silu(g_ref[:, cs])).astype(BF16)
    if emit_state:
        st_ref[...] = s_scr[...]


def _retention(p, dec_rep, s0, batch, seq, emit_state):
    n = batch * seq
    w = RET_HEADS * RET_DK
    has_s0 = s0 is not None
    kern = functools.partial(_retention_kernel, seq=seq, has_s0=has_s0, emit_state=emit_state)
    in_specs = [
        pl.BlockSpec((seq, w), lambda b: (b, 0)),
        pl.BlockSpec((seq, w), lambda b: (b, 1)),
        pl.BlockSpec((seq, w), lambda b: (b, 2)),
        pl.BlockSpec((seq, w), lambda b: (b, 3)),
        pl.BlockSpec((2, w), lambda b: (0, 0)),
    ]
    args = [p, p, p, p, dec_rep]
    if has_s0:
        in_specs.append(pl.BlockSpec((None, 2, RET_HEADS, RET_DK, RET_DV), lambda b: (b, 0, 0, 0, 0)))
        args.append(s0)
    out_specs = [pl.BlockSpec((seq, w), lambda b: (b, 0))]
    out_shape = [jax.ShapeDtypeStruct((n, w), BF16)]
    if emit_state:
        out_specs.append(pl.BlockSpec((None, 2, RET_HEADS, RET_DK, RET_DV), lambda b: (b, 0, 0, 0, 0)))
        out_shape.append(jax.ShapeDtypeStruct((batch, 2, RET_HEADS, RET_DK, RET_DV), F32))
    return pl.pallas_call(
        kern,
        grid=(batch,),
        in_specs=in_specs,
        out_specs=out_specs,
        out_shape=out_shape,
        scratch_shapes=[pltpu.VMEM((2, RET_HEADS, RET_DK, RET_DV), F32), pltpu.VMEM((seq, w), F32)],
        compiler_params=_cparams(("arbitrary",)),
        name="retention",
    )(*args)


ATT_QB = 256


def _head_rms(x, ones_bd, gain):
    ss = _dot_x2(x * x, ones_bd)
    return x * lax.rsqrt(ss * (1.0 / ATT_HD) + EPS) * gain


def _rope(x, cos, sin_signed):
    w = x.shape[-1]
    lane = lax.broadcasted_iota(jnp.int32, x.shape, 1)
    first = (lane & 16) == 0
    swapped = jnp.where(first, pltpu.roll(x, w - 16, axis=1), pltpu.roll(x, 16, axis=1))
    return x * cos + swapped * sin_signed


def _attention_kernel(*refs, seq, decode, past):
    q_ref, kv_ref, gq_ref, gk_ref, bdq_ref, bdk_ref = refs[:6]
    pos = 6
    if decode:
        ck_ref, cv_ref, cos_ref, sin_ref = refs[pos:pos + 4]
        pos += 4
    o_ref = refs[pos]
    pos += 1
    if not decode:
        ko_ref, vo_ref = refs[pos:pos + 2]
        pos += 2
    q_scr, k_scr, v_scr = refs[pos:pos + 3]

    kvw = ATT_KV_HEADS * ATT_HD
    q = _head_rms(q_ref[...], bdq_ref[...], gq_ref[...])
    kv = kv_ref[...]
    k = _head_rms(kv[:, :kvw], bdk_ref[...], gk_ref[...])
    v = kv[:, kvw:]
    if decode:
        cos = cos_ref[...]
        sin = sin_ref[...]
        q = _rope(q, cos, sin)
        k = _rope(k, cos[:, :kvw], sin[:, :kvw])
        k_scr[0:past, :] = ck_ref[...].astype(BF16)
        v_scr[0:past, :] = cv_ref[...].astype(BF16)
        k_scr[past:past + seq, :] = k.astype(BF16)
        v_scr[past:past + seq, :] = v.astype(BF16)
    else:
        ko_ref[...] = k
        vo_ref[...] = v
        k_scr[...] = k.astype(BF16)
        v_scr[...] = v.astype(BF16)
    q_scr[...] = (q * (ATT_HD ** -0.5)).astype(BF16)

    grp = ATT_HEADS // ATT_KV_HEADS

    def body(qi, carry):
        r0 = pl.multiple_of(qi * ATT_QB, ATT_QB)
        outs = []
        for h in range(ATT_HEADS):
            g = h // grp
            qh = q_scr[pl.ds(r0, ATT_QB), h * ATT_HD:(h + 1) * ATT_HD]
            kh = k_scr[:, g * ATT_HD:(g + 1) * ATT_HD]
            vh = v_scr[:, g * ATT_HD:(g + 1) * ATT_HD]
            s = lax.dot_general(qh, kh, (((1,), (1,)), ((), ())), preferred_element_type=F32)
            s = s - jnp.max(s, axis=-1, keepdims=True)
            e = jnp.exp(s)
            prob = e / jnp.sum(e, axis=-1, keepdims=True)
            outs.append(jnp.dot(prob.astype(BF16), vh, preferred_element_type=F32))
        o_ref[pl.ds(r0, ATT_QB), :] = jnp.concatenate(outs, axis=1).astype(BF16)
        return carry

    lax.fori_loop(0, seq // ATT_QB, body, 0)


def _attention(p, gq, gk, bdq, bdk, batch, seq, cache=None):
    n = batch * seq
    qw = ATT_HEADS * ATT_HD
    kvw = ATT_KV_HEADS * ATT_HD
    decode = cache is not None
    past = cache[0].shape[1] if decode else 0
    kern = functools.partial(_attention_kernel, seq=seq, decode=decode, past=past)
    in_specs = [
        pl.BlockSpec((seq, qw), lambda b: (b, T_AQ * COL_TILE // qw)),
        pl.BlockSpec((seq, 2 * kvw), lambda b: (b, T_AKV * COL_TILE // (2 * kvw))),
        pl.BlockSpec((1, qw), lambda b: (0, 0)),
        pl.BlockSpec((1, kvw), lambda b: (0, 0)),
        pl.BlockSpec((qw, qw), lambda b: (0, 0)),
        pl.BlockSpec((kvw, kvw), lambda b: (0, 0)),
    ]
    args = [p, p, gq, gk, bdq, bdk]
    out_specs = [pl.BlockSpec((seq, qw), lambda b: (b, 0))]
    out_shape = [jax.ShapeDtypeStruct((n, qw), BF16)]
    if decode:
        ck, cv, cos, sin = cache
        in_specs += [
            pl.BlockSpec((None, past, kvw), lambda b: (b, 0, 0)),
            pl.BlockSpec((None, past, kvw), lambda b: (b, 0, 0)),
            pl.BlockSpec((seq, qw), lambda b: (0, 0)),
            pl.BlockSpec((seq, qw), lambda b: (0, 0)),
        ]
        args += [ck, cv, cos, sin]
    else:
        out_specs += [pl.BlockSpec((None, seq, kvw), lambda b: (b, 0, 0))] * 2
        out_shape += [jax.ShapeDtypeStruct((batch, seq, kvw), F32)] * 2
    return pl.pallas_call(
        kern,
        grid=(batch,),
        in_specs=in_specs,
        out_specs=out_specs,
        out_shape=out_shape,
        scratch_shapes=[pltpu.VMEM((seq, qw), BF16), pltpu.VMEM((past + seq, kvw), BF16),
                        pltpu.VMEM((past + seq, kvw), BF16)],
        compiler_params=_cparams(("arbitrary",)),
        name="attention",
    )(*args)


def _rope_tables(seq):
    t = jnp.arange(seq, dtype=jnp.int32)
    row = (t // GRID_W).astype(F32)
    col = (t % GRID_W).astype(F32)
    nf = ATT_HD // 4
    inv = ROPE_THETA ** (-jnp.arange(nf, dtype=F32) / nf)
    a_row = row[:, None] * inv[None, :]
    a_col = col[:, None] * inv[None, :]
    ang = jnp.concatenate([a_row, a_row, a_col, a_col], axis=1)
    sign = jnp.concatenate([-jnp.ones((nf,), F32), jnp.ones((nf,), F32)] * 2)
    cos = jnp.tile(jnp.cos(ang), (1, ATT_HEADS))
    sin = jnp.tile(jnp.sin(ang) * sign[None, :], (1, ATT_HEADS))
    return cos, sin


def _dft_matrices(seq):
    two_l = 2 * seq
    f = jnp.arange(seq, dtype=jnp.int32)
    t = jnp.arange(seq, dtype=jnp.int32)
    m_fwd = (f[:, None] * t[None, :]) % two_l
    ang_fwd = m_fwd.astype(F32) * (math.pi / seq)
    alt_t = jnp.where(t % 2 == 0, 1.0, -1.0).astype(F32)
    fc = jnp.cos(ang_fwd)
    fs = jnp.where(f[:, None] == 0, alt_t[None, :], jnp.sin(ang_fwd))
    n_out = t + seq // 2
    m_inv = (n_out[:, None] * f[None, :]) % two_l
    ang_inv = m_inv.astype(F32) * (math.pi / seq)
    wgt = jnp.where(f == 0, 1.0, 2.0).astype(F32) / two_l
    alt_n = jnp.where(n_out % 2 == 0, 1.0, -1.0).astype(F32)
    gc = jnp.cos(ang_inv) * wgt[None, :]
    gs = jnp.where(f[None, :] == 0, alt_n[:, None] / two_l, jnp.sin(ang_inv) * wgt[None, :])
    out = []
    for m in (fc, fs, gc, gs):
        hi = m.astype(BF16)
        lo = (m - hi.astype(F32)).astype(BF16)
        out += [hi, lo]
    return tuple(out)


def _hyena_feats(seq):
    t = jnp.arange(seq, dtype=F32) / seq
    bands = jnp.arange(1, HY_BANDS + 1, dtype=F32)
    ang = 2.0 * jnp.pi * t[:, None] * bands[None, :]
    z = jnp.concatenate([t[:, None], jnp.sin(ang), jnp.cos(ang)], -1)
    return jnp.pad(z, ((0, 0), (0, LANES - HY_FEAT)))


def _hyena_filter_kernel(z_ref, w1_ref, b1_ref, w2_ref, b2_ref, fr_ref, w3_ref, lr_ref,
                         fch_ref, fcl_ref, fsh_ref, fsl_ref, hc_ref, hs_ref, *, seq):
    fr = fr_ref[...]
    h = jnp.sin(fr[0:1, :] * (_dot3(z_ref[...], w1_ref[...]) + b1_ref[...]))
    h = jnp.sin(fr[1:2, :] * (_dot3(h, w2_ref[...]) + b2_ref[...]))
    h = _dot3(h, w3_ref[...])
    t = lax.broadcasted_iota(jnp.int32, (seq, 1), 0).astype(F32)
    dist = jnp.abs(t - float(seq // 2)) / (0.5 * seq)
    h = h * jnp.exp(-jnp.exp(lr_ref[...]) * dist)
    hc_ref[...] = _dot3_pre(fch_ref[...], fcl_ref[...], h)
    hs_ref[...] = _dot3_pre(fsh_ref[...], fsl_ref[...], h)


def _hyena_filters(z, w1p, b1, w2, b2, fr, w3, lr, dft, seq):
    fch, fcl, fsh, fsl = dft[:4]
    full = lambda a: pl.BlockSpec(a.shape, lambda: (0,) * a.ndim)
    args = [z, w1p, b1, w2, b2, fr, w3, lr, fch, fcl, fsh, fsl]
    return pl.pallas_call(
        functools.partial(_hyena_filter_kernel, seq=seq),
        in_specs=[full(a) for a in args],
        out_specs=[pl.BlockSpec((seq, 2 * HY_C), lambda: (0, 0))] * 2,
        out_shape=[jax.ShapeDtypeStruct((seq, 2 * HY_C), F32)] * 2,
        compiler_params=pltpu.CompilerParams(vmem_limit_bytes=VMEM_LIMIT),
        name="hyena_filters",
    )(*args)


def _shift_rows(u, seq):
    row = lax.broadcasted_iota(jnp.int32, u.shape, 0)
    prev = jnp.where(row == 0, 0.0, pltpu.roll(u, 1, axis=0))
    nxt = jnp.where(row == seq - 1, 0.0, pltpu.roll(u, seq - 1, axis=0))
    return prev, nxt


def _hyena_kernel(v_ref, x1_ref, x2_ref, cv_ref, c1_ref, c2_ref, bias_ref,
                  hc0_ref, hs0_ref, hc1_ref, hs1_ref,
                  fc_ref, fs_ref, gc_ref, gs_ref, o_ref, *, seq):
    def dwconv(u_ref, w_ref):
        u = u_ref[...]
        w = w_ref[...]
        prev, nxt = _shift_rows(u, seq)
        return prev * w[0:1, :] + u * w[1:2, :] + nxt * w[2:3, :]

    row0 = lax.broadcasted_iota(jnp.int32, (seq, COL_TILE), 0) == 0

    def long_conv(u, hc, hs):
        u16 = u.astype(BF16)
        uc = jnp.dot(fc_ref[...], u16, preferred_element_type=F32)
        us = jnp.dot(fs_ref[...], u16, preferred_element_type=F32)
        ss = us * hs
        yc = uc * hc - jnp.where(row0, 0.0, ss)
        ys = jnp.where(row0, ss, uc * hs + us * hc)
        return (jnp.dot(gc_ref[...], yc.astype(BF16), preferred_element_type=F32)
                + jnp.dot(gs_ref[...], ys.astype(BF16), preferred_element_type=F32))

    bias = bias_ref[...]
    v = dwconv(v_ref, cv_ref)
    z = long_conv(v, hc0_ref[...], hs0_ref[...]) + bias[0:1, :] * v
    z = dwconv(x1_ref, c1_ref) * z
    z = long_conv(z, hc1_ref[...], hs1_ref[...]) + bias[1:2, :] * z
    o_ref[...] = (dwconv(x2_ref, c2_ref) * z).astype(BF16)


def _hyena(p, hy_conv, hy_bias, hc, hs, dft, layer, batch, seq):
    n = batch * seq
    nb = HY_C // COL_TILE
    tile = lambda off: pl.BlockSpec((seq, COL_TILE), lambda b, c: (b, T_HY + off * nb + c))
    cw = lambda off: pl.BlockSpec((None, 3, COL_TILE), lambda b, c: (layer, 0, off * nb + c))
    filt = lambda o: pl.BlockSpec((seq, COL_TILE), lambda b, c: (0, o * nb + c))
    mat = pl.BlockSpec((seq, seq), lambda b, c: (0, 0))
    return pl.pallas_call(
        functools.partial(_hyena_kernel, seq=seq),
        grid=(batch, nb),
        in_specs=[tile(0), tile(1), tile(2), cw(0), cw(1), cw(2),
                  pl.BlockSpec((None, 2, COL_TILE), lambda b, c: (layer, 0, c)),
                  filt(0), filt(0), filt(1), filt(1)] + [mat] * 4,
        out_specs=pl.BlockSpec((seq, COL_TILE), lambda b, c: (b, c)),
        out_shape=jax.ShapeDtypeStruct((n, HY_C), BF16),
        compiler_params=_cparams(("arbitrary", "arbitrary")),
        name="hyena",
    )(p, p, p, hy_conv, hy_conv, hy_conv, hy_bias, hc, hs, hc, hs, dft[0], dft[2], dft[4], dft[6])


def _stack2(x):
    lane = lax.broadcasted_iota(jnp.int32, x.shape, 1)
    first = lane < RW_N
    return jnp.concatenate([jnp.where(first, x, 0.0), jnp.where(first, 0.0, x)], axis=0)


def _rwkv_kernel(*refs, seq, has_s0):
    (pr_ref, pk_ref, pv_ref, pl_ref, mur_ref, muk_ref, muv_ref, mul_ref, aup_ref, gup_ref, wup_ref, w0_ref,
     a0_ref, kk_ref, ka_ref, rk_ref, bd_ref) = refs[:17]
    pos = 17
    s0_ref = None
    if has_s0:
        s0_ref = refs[pos]
        pos += 1
    o_ref, st_ref = refs[pos], refs[pos + 1]
    r_s, k_s, v_s, a_s, b_s, gate_s, bonus_s, y_scr, e_s, pre16, pre32 = refs[pos + 2:pos + 13]

    def shifted(ref, mu_ref):
        p = ref[...]
        prev, nxt = _shift_rows(p, seq)
        return p + mu_ref[...] * (0.5 * (prev + nxt) - p)

    bd = bd_ref[...]
    r = shifted(pr_ref, mur_ref)
    k = shifted(pk_ref, muk_ref)
    v = shifted(pv_ref, muv_ref)
    low = shifted(pl_ref, mul_ref)
    wa = low[:, :LANES]
    iclr = jax.nn.sigmoid(a0_ref[...] + _dot(wa, aup_ref[...]))
    gate_s[...] = _dot(jax.nn.sigmoid(low[:, LANES:]), gup_ref[...])
    kk = k * kk_ref[...]
    kk = kk * lax.rsqrt(_dot_x2(kk * kk, bd) + 1e-12)
    k2 = k * (1.0 + (iclr - 1.0) * ka_ref[...])
    bonus_s[...] = _dot_x2(r * k2 * rk_ref[...], bd) * v
    r_s[...] = r
    k_s[...] = k2
    v_s[...] = v
    a_s[...] = -kk
    b_s[...] = kk * iclr
    tw = jnp.tanh(wa)
    for d in range(2):
        x = -(w0_ref[d:d + 1, :] + _dot(tw, wup_ref[d]))
        softplus = jnp.maximum(x, 0.0) + jnp.log1p(jnp.exp(-jnp.abs(x)))
        e_s[d] = jnp.exp(-softplus - 0.5)

    C = RW_CHUNK
    S = 2 * C
    nc = seq // C
    ri = lax.broadcasted_iota(jnp.int32, (S, S), 0)
    ci = lax.broadcasted_iota(jnp.int32, (S, S), 1)
    eye = ri == ci
    eye_f = jnp.where(eye, 1.0, 0.0)
    tr = ri & (C - 1)
    tc = ci & (C - 1)
    ti = lax.broadcasted_iota(jnp.int32, (C, C), 0)
    tj = lax.broadcasted_iota(jnp.int32, (C, C), 1)
    dir_consts = (
        (tc < tr, tc <= tr, (tj <= ti).astype(BF16), C - 1),
        (tc > tr, tc >= tr, (tj >= ti).astype(BF16), 0),
    )
    level_masks = []
    m = 1
    while m < C:
        lg = m.bit_length() - 1
        same = jnp.right_shift(tr, lg + 1) == jnp.right_shift(tc, lg + 1)
        halves = (jnp.right_shift(tr, lg) & 1) != (jnp.right_shift(tc, lg) & 1)
        level_masks.append(same & halves)
        m *= 2

    def bf(x):
        return x.astype(BF16)

    def mm(x, y):
        return jnp.dot(x, y, preferred_element_type=F32)

    def mm_nt(x, y):
        return lax.dot_general(x, y, (((1,), (1,)), ((), ())), preferred_element_type=F32)

    def mm_tn(x, y):
        return lax.dot_general(x, y, (((0,), (0,)), ((), ())), preferred_element_type=F32)

    def phase1(units):
        fr = []
        row_sl = [pl.ds(pl.multiple_of(c * C, C), C) for _, c in units]
        ecs = [e_s[d, rows, :] for (d, _), rows in zip(units, row_sl)]
        cums = [_dot_m2x(dir_consts[d][2], ec) for (d, _), ec in zip(units, ecs)]
        for (d, c), rows, ec, cum in zip(units, row_sl, ecs, cums):
            strict, incl, tri, last = dir_consts[d]
            g_c = cum[last:last + 1, :]
            inv = jnp.exp(cum)
            to_end = jnp.exp(cum - g_c)
            a16 = bf(_stack2(a_s[rows, :] * jnp.exp(ec - cum)))
            r_t = _stack2(r_s[rows, :] * jnp.exp(-cum))
            bc = b_s[rows, :]
            kc = k_s[rows, :]
            gm = mm_nt(jnp.concatenate([a16, bf(r_t)], axis=0),
                       jnp.concatenate([bf(_stack2(bc * inv)), bf(_stack2(kc * inv))], axis=0))
            fr.append(dict(
                d=d, c=c, g_c=g_c, a16=a16, r_t=r_t,
                bh16=bf(_stack2(bc * to_end)), kh16=bf(_stack2(kc * to_end)), v16=bf(_stack2(v_s[rows, :])),
                n_m=jnp.where(strict, gm[:S, :S], 0.0),
                ak16=bf(jnp.where(strict, gm[:S, S:], 0.0)),
                rb16=bf(jnp.where(incl, gm[S:, :S], 0.0)),
                rk16=bf(jnp.where(incl, gm[S:, S:], 0.0))))
        nu = len(fr)
        ts = [None] * nu
        for li, mask in enumerate(level_masks):
            if li == 0:
                ts = [eye_f + jnp.where(mask, f['n_m'], 0.0) for f in fr]
                continue
            t16 = [bf(t) for t in ts]
            inner = [mm(bf(jnp.where(mask, fr[u]['n_m'], 0.0)), t16[u]) for u in range(nu)]
            ts = [ts[u] + mm(t16[u], bf(inner[u])) for u in range(nu)]
        akv = [mm(f['ak16'], f['v16']) for f in fr]
        wu = [mm(bf(ts[u]), jnp.concatenate([fr[u]['a16'], bf(akv[u])], axis=1)) for u in range(nu)]
        w16 = [bf(x[:, :S]) for x in wu]
        uv16 = [jnp.concatenate([bf(wu[u][:, S:]), fr[u]['v16']], axis=0) for u in range(nu)]
        for u, f in enumerate(fr):
            d, c = f['d'], f['c']
            pre16[d, c, 0] = bf(f['r_t'] + mm(f['rb16'], w16[u]))
            pre32[d, c, 0] = mm(jnp.concatenate([f['rb16'], f['rk16']], axis=1), uv16[u])
            pre16[d, c, 1] = bf(jnp.where(eye, jnp.exp(-f['g_c']), 0.0) + mm_tn(w16[u], f['bh16']))
            pre32[d, c, 1] = mm_tn(uv16[u], jnp.concatenate([f['bh16'], f['kh16']], axis=0))

    group = min(nc, 4)
    if nc == group:
        phase1([(d, c) for d in range(2) for c in range(nc)])
    else:
        def phase1_body(i, carry):
            phase1([(d, i * group + j) for d in range(2) for j in range(group)])
            return carry

        lax.fori_loop(0, nc // group, phase1_body, 0)

    for d in range(2):
        st_ref[d] = s0_ref[d] if has_s0 else jnp.zeros((S, S), F32)
    y_scr[...] = jnp.zeros_like(y_scr)

    def phase2(i):
        cs = (i, nc - 1 - i)
        s16 = [bf(st_ref[d]) for d in range(2)]
        for d in range(2):
            st_ref[d] = mm(s16[d], pre16[d, cs[d], 1]) + pre32[d, cs[d], 1]
        for d in range(2):
            rows = pl.ds(pl.multiple_of(cs[d] * C, C), C)
            y_st = mm_nt(pre16[d, cs[d], 0], s16[d]) + pre32[d, cs[d], 0]
            y_scr[rows, :] = y_scr[rows, :] + (y_st[:C, :] + y_st[C:, :])

    if nc == group:
        for i in range(nc):
            phase2(i)
    else:
        def phase2_body(i, carry):
            phase2(i)
            return carry

        lax.fori_loop(0, nc, phase2_body, 0)

    y = y_scr[...]
    ones_bd = bd_ref[...]
    mean = _dot_x2(y, ones_bd) * (1.0 / RW_N)
    yc = y - mean
    var = _dot_x2(yc * yc, ones_bd) * (1.0 / RW_N)
    yn = yc * lax.rsqrt(var + RW_LN_EPS)
    o_ref[...] = ((yn + bonus_s[...]) * gate_s[...]).astype(BF16)


def _rwkv(p, mu, aup, gup, wup, w0, a0, kk, ka, rk, bd2, s0, batch, seq):
    n = batch * seq
    npair = RW_HEADS // 2
    has_s0 = s0 is not None
    S = 2 * RW_CHUNK
    base = T_RW * COL_TILE // LANES
    per = RW_C // LANES
    pcol = lambda sec: pl.BlockSpec((seq, LANES), lambda b, h: (b, base + sec * per + h))
    mucol = lambda sec: pl.BlockSpec((1, LANES), lambda b, h: (0, sec * per + h))
    vec = pl.BlockSpec((1, LANES), lambda b, h: (0, h))
    low_idx = 3 * RW_C // COL_TILE
    in_specs = [
        pcol(0), pcol(1), pcol(2),
        pl.BlockSpec((seq, COL_TILE), lambda b, h: (b, T_RW + low_idx)),
        mucol(0), mucol(1), mucol(2),
        pl.BlockSpec((1, COL_TILE), lambda b, h: (0, low_idx)),
        pl.BlockSpec((LANES, LANES), lambda b, h: (0, h)),
        pl.BlockSpec((LANES, LANES), lambda b, h: (0, h)),
        pl.BlockSpec((2, LANES, LANES), lambda b, h: (0, 0, h)),
        pl.BlockSpec((2, LANES), lambda b, h: (0, h)),
        vec, vec, vec, vec,
        pl.BlockSpec((LANES, LANES), lambda b, h: (0, 0)),
    ]
    args = [p, p, p, p, mu, mu, mu, mu, aup, gup, wup, w0, a0, kk, ka, rk, bd2]
    if has_s0:
        in_specs.append(pl.BlockSpec((None, 2, None, S, S), lambda b, h: (b, 0, h, 0, 0)))
        args.append(s0)
    seq_buf = pltpu.VMEM((seq, LANES), F32)
    return pl.pallas_call(
        functools.partial(_rwkv_kernel, seq=seq, has_s0=has_s0),
        grid=(batch, npair),
        in_specs=in_specs,
        out_specs=[pl.BlockSpec((seq, LANES), lambda b, h: (b, h)),
                   pl.BlockSpec((None, 2, None, S, S), lambda b, h: (b, 0, h, 0, 0))],
        out_shape=[jax.ShapeDtypeStruct((n, RW_C), BF16), jax.ShapeDtypeStruct((batch, 2, npair, S, S), F32)],
        scratch_shapes=[seq_buf] * 8 + [pltpu.VMEM((2, seq, LANES), F32),
                                        pltpu.VMEM((2, seq // RW_CHUNK, 2, S, S), BF16),
                                        pltpu.VMEM((2, seq // RW_CHUNK, 2, S, S), F32)],
        compiler_params=_cparams(("arbitrary", "arbitrary")),
        name="rwkv7",
    )(*args)


def _pair_states(s):
    b = s.shape[0]
    s = s.reshape(b, 2, RW_HEADS // 2, 2, RW_N, RW_N)
    z = jnp.zeros_like(s[:, :, :, 0])
    top = jnp.concatenate([s[:, :, :, 0], z], axis=-1)
    bot = jnp.concatenate([z, s[:, :, :, 1]], axis=-1)
    return jnp.concatenate([top, bot], axis=-2)


def _unpair_states(s):
    b = s.shape[0]
    h0 = s[:, :, :, :RW_N, :RW_N]
    h1 = s[:, :, :, RW_N:, RW_N:]
    return jnp.stack([h0, h1], axis=3).reshape(b, 2, RW_HEADS, RW_N, RW_N)


MRG_TM = 1024
MRG_TN = 256


def _merge_kernel(h_ref, ba_ref, bb_ref, bc_ref, bd_ref, g0_ref, g1_ref, g2_ref, g3_ref, wb_ref, o_ref):
    h = h_ref[...]
    acc = None
    for i, (br, gw) in enumerate(((ba_ref, g0_ref), (bb_ref, g1_ref), (bc_ref, g2_ref), (bd_ref, g3_ref))):
        gate = jax.nn.sigmoid(jnp.dot(h, gw[...].astype(BF16), preferred_element_type=F32))
        proj = jnp.dot(br[...], wb_ref[i].astype(BF16), preferred_element_type=F32)
        acc = gate * proj if acc is None else acc + gate * proj
    o_ref[...] = acc.astype(BF16)


def _merge(h, branches, w_in, w_branch, layer):
    n = h.shape[0]
    gate_spec = lambda b: pl.BlockSpec(
        (None, D_MODEL, MRG_TN), lambda i, j: (layer, 0, (MIX_W + b * D_MODEL) // MRG_TN + j))
    return pl.pallas_call(
        _merge_kernel,
        grid=(n // MRG_TM, D_MODEL // MRG_TN),
        in_specs=[pl.BlockSpec((MRG_TM, D_MODEL), lambda i, j: (i, 0))]
        + [pl.BlockSpec((MRG_TM, BRANCH_W), lambda i, j: (i, 0))] * N_BRANCH
        + [gate_spec(b) for b in range(N_BRANCH)]
        + [pl.BlockSpec((None, N_BRANCH, BRANCH_W, MRG_TN), lambda i, j: (layer, 0, 0, j))],
        out_specs=pl.BlockSpec((MRG_TM, MRG_TN), lambda i, j: (i, j)),
        out_shape=jax.ShapeDtypeStruct((n, D_MODEL), BF16),
        compiler_params=_cparams(("arbitrary", "arbitrary")),
        name="gated_merge",
    )(h, *branches, w_in, w_in, w_in, w_in, w_branch)


OUT_TM = 256


def _route(logits):
    lane_i = lax.broadcasted_iota(jnp.int32, logits.shape, 1)
    lane = lane_i.astype(F32)
    grp_of_lane = jnp.right_shift(lane_i, MOE_PER_GROUP.bit_length() - 1).astype(F32)
    neg = -jnp.inf
    is_grp = (lane_i >= MOE_EXPERTS) & (lane_i < MOE_EXPERTS + MOE_GROUPS)
    gl = jnp.where(is_grp, logits, neg)
    gmax = jnp.max(gl, axis=-1, keepdims=True)
    gsel = jnp.min(jnp.where(gl == gmax, lane, float(LANES)), axis=-1, keepdims=True) - float(MOE_EXPERTS)
    gval = 1.0 / jnp.sum(jnp.exp(gl - gmax), axis=-1, keepdims=True)
    in_grp = (grp_of_lane == gsel) & (lane_i < MOE_EXPERTS)
    el = jnp.where(in_grp, logits, neg)
    ee = jnp.exp(el - jnp.max(el, axis=-1, keepdims=True))
    prob = jnp.where(in_grp, ee / jnp.sum(ee, axis=-1, keepdims=True), -1.0)
    v1 = jnp.max(prob, axis=-1, keepdims=True)
    i1 = jnp.min(jnp.where(prob == v1, lane, float(LANES)), axis=-1, keepdims=True)
    prob2 = jnp.where(lane == i1, -1.0, prob)
    v2 = jnp.max(prob2, axis=-1, keepdims=True)
    i2 = jnp.min(jnp.where(prob2 == v2, lane, float(LANES)), axis=-1, keepdims=True)
    tot = v1 + v2
    comb = jnp.where(lane == i1, gval * (v1 / tot), 0.0) + jnp.where(lane == i2, gval * (v2 / tot), 0.0)
    comb = comb + jnp.where(lane_i == LANES - 1, gsel, 0.0)
    return comb, jnp.where(lane == gsel, 1.0, 0.0)


def _outproj_kernel(m_ref, w_ref, x_ref, mod_ref, lng_ref, lnb_ref, rwh_ref, rwl_ref, rb_ref,
                    x1_ref, h2_ref, comb_ref, cnt_ref, *, req_base, rows_per_req):
    i = pl.program_id(0)
    mix = jnp.dot(m_ref[...], w_ref[...], preferred_element_type=F32)
    m = req_base + (i * OUT_TM) // rows_per_req
    g1 = mod_ref[2, pl.ds(m, 1), :]
    sh2 = mod_ref[3, pl.ds(m, 1), :]
    sc2 = mod_ref[4, pl.ds(m, 1), :]
    x1 = _ln(ALPHA * x_ref[...] + g1 * mix) * lng_ref[0:1, :] + lnb_ref[0:1, :]
    x1_ref[...] = x1
    h2 = _ln(x1) * (1.0 + sc2) + sh2
    h2_ref[...] = h2.astype(BF16)
    hh, hl = _split(h2)
    rwh = rwh_ref[...]
    logits = (jnp.dot(hh, rwh, preferred_element_type=F32) + jnp.dot(hl, rwh, preferred_element_type=F32)
              + jnp.dot(hh, rwl_ref[...], preferred_element_type=F32)) + rb_ref[...]
    comb, grp_onehot = _route(logits)
    comb_ref[...] = comb
    cnt_ref[...] = jnp.broadcast_to(jnp.sum(grp_onehot, axis=0, keepdims=True), cnt_ref.shape)


def _outproj(merged, w_out16, x2d, mod_l, ln_g, ln_b, rwh, rwl, rb, layer, req_base, rows_per_req):
    n = x2d.shape[0]
    kern = functools.partial(_outproj_kernel, req_base=req_base, rows_per_req=rows_per_req)
    row = lambda w: pl.BlockSpec((OUT_TM, w), lambda i: (i, 0))
    full = lambda a: pl.BlockSpec(a.shape, lambda i: (0,) * a.ndim)
    lnspec = pl.BlockSpec((None, 2, D_MODEL), lambda i: (layer, 0, 0))
    return pl.pallas_call(
        kern,
        grid=(n // OUT_TM,),
        in_specs=[row(D_MODEL),
                  pl.BlockSpec((None, D_MODEL, D_MODEL), lambda i: (layer, 0, 0)),
                  row(D_MODEL), full(mod_l), lnspec, lnspec, full(rwh), full(rwl), full(rb)],
        out_specs=[row(D_MODEL), row(D_MODEL), row(LANES), pl.BlockSpec((None, 8, LANES), lambda i: (i, 0, 0))],
        out_shape=[jax.ShapeDtypeStruct((n, D_MODEL), F32), jax.ShapeDtypeStruct((n, D_MODEL), BF16),
                   jax.ShapeDtypeStruct((n, LANES), F32), jax.ShapeDtypeStruct((n // OUT_TM, 8, LANES), F32)],
        compiler_params=_cparams(("arbitrary",)),
        name="out_projection",
    )(merged, w_out16, x2d, mod_l, ln_g, ln_b, rwh, rwl, rb)


MOE_TM = 1024
MOE_RB = 256
MOE_CB = 512


def _moe_kernel(offs_ref, h_ref, comb_ref, wg_ref, wu_ref, wd_ref, o_ref, xs_scr, cs_scr, pm_scr, wg16, wu16, wd16):
    i = pl.program_id(0)
    e = pl.program_id(1)

    @pl.when(e == 0)
    def _():
        comb = comb_ref[...]
        lane_f = lax.broadcasted_iota(jnp.int32, comb.shape, 1).astype(F32)
        onehot = jnp.where(lane_f == comb[:, LANES - 1:LANES], 1.0, 0.0)
        r_t = lax.broadcasted_iota(jnp.int32, (MOE_TM, MOE_TM), 0)
        c_t = lax.broadcasted_iota(jnp.int32, (MOE_TM, MOE_TM), 1)
        tri = (c_t <= r_t).astype(BF16)
        rank = jnp.dot(tri, onehot.astype(BF16), preferred_element_type=F32)
        cnt = rank[MOE_TM - 1:MOE_TM, :]
        r_g = lax.broadcasted_iota(jnp.int32, (LANES, LANES), 0)
        c_g = lax.broadcasted_iota(jnp.int32, (LANES, LANES), 1)
        start = _dot_x2(cnt, (r_g < c_g).astype(BF16))
        pos = jnp.sum(onehot * (start + rank - 1.0), axis=-1, keepdims=True)
        pm = jnp.where(c_t.astype(F32) == pos, 1.0, 0.0).astype(BF16)
        pm_scr[...] = pm
        for cb in range(D_MODEL // MOE_CB):
            cols = slice(cb * MOE_CB, (cb + 1) * MOE_CB)
            xs_scr[:, cols] = lax.dot_general(pm, h_ref[:, cols], (((0,), (0,)), ((), ())),
                                              preferred_element_type=F32).astype(BF16)
        c_hi = comb.astype(BF16)
        c_mid = (comb - c_hi.astype(F32)).astype(BF16)
        c_lo = (comb - c_hi.astype(F32) - c_mid.astype(F32)).astype(BF16)
        tn = lambda y: lax.dot_general(pm, y, (((0,), (0,)), ((), ())), preferred_element_type=F32)
        cs_scr[...] = tn(c_hi) + tn(c_mid) + tn(c_lo)
        o_ref[...] = jnp.zeros_like(o_ref)

    g = e // MOE_PER_GROUP
    lo = offs_ref[i, g]
    hi = offs_ref[i, g + 1]

    @pl.when(hi > lo)
    def _():
        wg16[...] = wg_ref[...].astype(BF16)
        wu16[...] = wu_ref[...].astype(BF16)
        wd16[...] = wd_ref[...].astype(BF16)

        def block(j, carry):
            rows = pl.ds(pl.multiple_of(j * MOE_RB, MOE_RB), MOE_RB)
            x = xs_scr[rows, :]
            cs = cs_scr[rows, :]
            lane = lax.broadcasted_iota(jnp.int32, cs.shape, 1)
            ce = jnp.sum(jnp.where(lane == e, cs, 0.0), axis=-1, keepdims=True)
            hg = jnp.dot(x, wg16[...], preferred_element_type=F32)
            hu = jnp.dot(x, wu16[...], preferred_element_type=F32)
            act = _silu(hg) * hu * ce
            o_ref[rows, :] += jnp.dot(act.astype(BF16), wd16[...], preferred_element_type=F32)
            return carry

        lax.fori_loop(lo // MOE_RB, (hi + MOE_RB - 1) // MOE_RB, block, 0)

    @pl.when(e == pl.num_programs(1) - 1)
    def _():
        pm = pm_scr[...]
        for cb in range(D_MODEL // MOE_CB):
            cols = slice(cb * MOE_CB, (cb + 1) * MOE_CB)
            o_ref[:, cols] = _dot_m2x(pm, o_ref[:, cols])


def _moe(h2, comb, offs, w_gate, w_up, w_down, layer):
    n = h2.shape[0]
    grid_spec = pltpu.PrefetchScalarGridSpec(
        num_scalar_prefetch=1,
        grid=(n // MOE_TM, MOE_EXPERTS),
        in_specs=[pl.BlockSpec((MOE_TM, D_MODEL), lambda i, e, offs: (i, 0)),
                  pl.BlockSpec((MOE_TM, LANES), lambda i, e, offs: (i, 0)),
                  pl.BlockSpec((None, None, D_MODEL, MOE_HID), lambda i, e, offs: (layer, e, 0, 0)),
                  pl.BlockSpec((None, None, D_MODEL, MOE_HID), lambda i, e, offs: (layer, e, 0, 0)),
                  pl.BlockSpec((None, None, MOE_HID, D_MODEL), lambda i, e, offs: (layer, e, 0, 0))],
        out_specs=pl.BlockSpec((MOE_TM, D_MODEL), lambda i, e, offs: (i, 0)),
        scratch_shapes=[pltpu.VMEM((MOE_TM, D_MODEL), BF16), pltpu.VMEM((MOE_TM, LANES), F32),
                        pltpu.VMEM((MOE_TM, MOE_TM), BF16), pltpu.VMEM((D_MODEL, MOE_HID), BF16),
                        pltpu.VMEM((D_MODEL, MOE_HID), BF16), pltpu.VMEM((MOE_HID, D_MODEL), BF16)])
    return pl.pallas_call(
        _moe_kernel,
        grid_spec=grid_spec,
        out_shape=jax.ShapeDtypeStruct((n, D_MODEL), F32),
        compiler_params=_cparams(("arbitrary", "arbitrary")),
        name="moe_experts",
    )(offs, h2, comb, w_gate, w_up, w_down)


def _group_offsets(cnt):
    per = MOE_TM // OUT_TM
    c = cnt[:, 0, :MOE_GROUPS].astype(jnp.int32).reshape(-1, per, MOE_GROUPS).sum(axis=1)
    return jnp.concatenate([jnp.zeros((c.shape[0], 1), jnp.int32), jnp.cumsum(c, axis=1)], axis=1)


FIN_TM = 512


def _final_kernel(x_ref, f_ref, mod_ref, lng_ref, lnb_ref, o_ref, *, req_base, rows_per_req):
    i = pl.program_id(0)
    m = req_base + (i * FIN_TM) // rows_per_req
    g2 = mod_ref[5, pl.ds(m, 1), :]
    o_ref[...] = _ln(ALPHA * x_ref[...] + g2 * f_ref[...]) * lng_ref[1:2, :] + lnb_ref[1:2, :]


def _final(x1, ffn, mod_l, ln_g, ln_b, layer, req_base, rows_per_req):
    n = x1.shape[0]
    row = pl.BlockSpec((FIN_TM, D_MODEL), lambda i: (i, 0))
    lnspec = pl.BlockSpec((None, 2, D_MODEL), lambda i: (layer, 0, 0))
    return pl.pallas_call(
        functools.partial(_final_kernel, req_base=req_base, rows_per_req=rows_per_req),
        grid=(n // FIN_TM,),
        in_specs=[row, row, pl.BlockSpec(mod_l.shape, lambda i: (0, 0, 0)), lnspec, lnspec],
        out_specs=row,
        out_shape=jax.ShapeDtypeStruct((n, D_MODEL), F32),
        compiler_params=_cparams(("arbitrary",)),
        name="final_norm",
    )(x1, ffn, mod_l, ln_g, ln_b)


def _block_diag_ones(width, blk):
    i = jnp.arange(width) // blk
    return (i[:, None] == i[None, :]).astype(BF16)


def _layer(x2d, layer, batch, seq, req_base, rows_per_req, mod_l, wts, consts, ctx):
    decode = ctx is not None
    p, h = _inproj(x2d, mod_l, wts['w_in'], layer, req_base, rows_per_req)

    ret = _retention(p, consts['ret_dec'][layer], ctx['ret'] if decode else None, batch, seq,
                     emit_state=not decode)
    cache = (ctx['k'], ctx['v'], consts['rope_cos'], consts['rope_sin']) if decode else None
    att = _attention(p, consts['gq'][layer], consts['gk'][layer], consts['bd_q'], consts['bd_k'],
                     batch, seq, cache)
    hc, hs = consts['hy_filt'][seq][layer]
    o_c = _hyena(p, wts['hy_conv'], wts['hy_bias'], hc, hs, consts['dft'][seq], layer, batch, seq)
    o_d, rw_state = _rwkv(p, consts['rw_mu'][layer], consts['rw_aup'][layer], wts['rw_g_up'][layer],
                          consts['rw_wup'][layer], wts['rw_w0'][layer], consts['rw_a0'][layer],
                          consts['rw_kk'][layer], consts['rw_ka'][layer], consts['rw_rk'][layer],
                          consts['bd_pair'], ctx['rw'] if decode else None, batch, seq)

    merged = _merge(h, (ret[0], att[0], o_c, o_d), wts['w_in'], wts['w_branch'], layer)
    x1, h2, comb, cnt = _outproj(merged, consts['w_out16'], x2d, mod_l, wts['ln_g'], wts['ln_b'],
                            consts['router_hi'][layer], consts['router_lo'][layer], consts['router_b'][layer],
                            layer, req_base, rows_per_req)
    ffn = _moe(h2, comb, _group_offsets(cnt), wts['moe_w_gate'], wts['moe_w_up'], wts['moe_w_down'], layer)
    x2 = _final(x1, ffn, mod_l, wts['ln_g'], wts['ln_b'], layer, req_base, rows_per_req)
    if decode:
        return x2, None
    return x2, (att[1], att[2], ret[1], rw_state)


def kernel(x_prompt, x_sample, c, cache_attn_k, cache_attn_v, state_ret, state_rwkv, c_ctx, mod_w, mod_b, w_in, ret_decay_exp, attn_q_norm, attn_k_norm, hy_conv, hy_w1, hy_b1, hy_w2, hy_b2, hy_freq, hy_w3, hy_log_rate, hy_bias, rw_mu, rw_w0, rw_w_up, rw_a0, rw_a_up, rw_g_up, rw_k_k, rw_k_a, rw_r_k, w_branch, w_out, ln_g, ln_b, moe_rg_w, moe_rg_b, moe_re_w, moe_re_b, moe_w_gate, moe_w_up, moe_w_down):
    batch, seq, _ = x_prompt.shape
    dbatch, dseq, _ = x_sample.shape
    past = cache_attn_k.shape[2]
    kvw = ATT_KV_HEADS * ATT_HD

    cvec = jnp.concatenate([c_ctx[None, :], c, jnp.zeros((8 - 1 - dbatch, D_MODEL), F32)], axis=0)
    mod = _modulation(cvec, mod_w, mod_b)

    wts = dict(w_in=w_in, hy_conv=hy_conv, hy_bias=hy_bias, rw_g_up=rw_g_up, rw_w0=rw_w0, w_branch=w_branch,
               ln_g=ln_g, ln_b=ln_b, moe_w_gate=moe_w_gate, moe_w_up=moe_w_up,
               moe_w_down=moe_w_down)

    zlo = jnp.zeros((DEPTH, RW_N, RW_C), F32)
    router_pad = LANES - MOE_EXPERTS - MOE_GROUPS
    router_w = jnp.pad(jnp.concatenate([moe_re_w, moe_rg_w], axis=2), ((0, 0), (0, 0), (0, router_pad)))
    router_hi = router_w.astype(BF16)
    router_lo = (router_w - router_hi.astype(F32)).astype(BF16)
    rope_cos, rope_sin = _rope_tables(dseq)
    consts = dict(
        ret_dec=jnp.repeat(ret_decay_exp, RET_DV, axis=-1),
        gq=jnp.tile(attn_q_norm, (1, ATT_HEADS))[:, None, :],
        gk=jnp.tile(attn_k_norm, (1, ATT_KV_HEADS))[:, None, :],
        bd_q=_block_diag_ones(ATT_HEADS * ATT_HD, ATT_HD),
        bd_k=_block_diag_ones(kvw, ATT_HD),
        bd_pair=_block_diag_ones(LANES, RW_N),
        rope_cos=rope_cos, rope_sin=rope_sin,
        rw_mu=rw_mu[:, None, :],
        rw_aup=jnp.concatenate([zlo, rw_a_up], axis=1),
        rw_wup=jnp.concatenate([rw_w_up, jnp.zeros_like(rw_w_up)], axis=2),
        rw_a0=rw_a0[:, None, :], rw_kk=rw_k_k[:, None, :], rw_ka=rw_k_a[:, None, :],
        rw_rk=rw_r_k.reshape(DEPTH, 1, RW_C),
        w_out16=w_out.astype(BF16),
        router_hi=router_hi, router_lo=router_lo,
        router_b=jnp.pad(jnp.concatenate([moe_re_b, moe_rg_b], axis=1), ((0, 0), (0, router_pad)))[:, None, :],
        dft={}, hy_filt={},
    )
    w1p = jnp.pad(hy_w1, ((0, 0), (0, LANES - HY_FEAT), (0, 0)))
    for s in sorted({seq, dseq}):
        dft = _dft_matrices(s)
        z = _hyena_feats(s)
        consts['dft'][s] = dft
        consts['hy_filt'][s] = [
            _hyena_filters(z, w1p[l], hy_b1[l][None, :], hy_w2[l], hy_b2[l][None, :], hy_freq[l], hy_w3[l],
                           hy_log_rate[l].reshape(1, 2 * HY_C), dft, s)
            for l in range(DEPTH)]

    y = x_prompt.reshape(batch * seq, D_MODEL)
    ks, vs, rets, rws = [], [], [], []
    for l in range(DEPTH):
        y, (k_l, v_l, ret_l, rw_l) = _layer(y, l, batch, seq, 0, batch * seq, mod[l], wts, consts, None)
        ks.append(k_l)
        vs.append(v_l)
        rets.append(ret_l)
        rws.append(_unpair_states(rw_l))
    y_prompt = y.reshape(batch, seq, D_MODEL)
    new_k = jnp.stack(ks, 1).reshape(batch, DEPTH, seq, ATT_KV_HEADS, ATT_HD)
    new_v = jnp.stack(vs, 1).reshape(batch, DEPTH, seq, ATT_KV_HEADS, ATT_HD)
    new_ret = jnp.stack(rets, 1)
    new_rw = jnp.stack(rws, 1)

    ys = x_sample.reshape(dbatch * dseq, D_MODEL)
    for l in range(DEPTH):
        ctx = dict(k=cache_attn_k[:, l].reshape(dbatch, past, kvw), v=cache_attn_v[:, l].reshape(dbatch, past, kvw),
                   ret=state_ret[:, l], rw=_pair_states(state_rwkv[:, l]))
        ys, _ = _layer(ys, l, dbatch, dseq, 1, dseq, mod[l], wts, consts, ctx)
    y_sample = ys.reshape(dbatch, dseq, D_MODEL)

    return (y_prompt, y_sample, new_k, new_v, new_ret, new_rw)
```

```python
import functools
import math

import jax
import jax.numpy as jnp
from jax import lax
from jax.experimental import pallas as pl
from jax.experimental.pallas import tpu as pltpu

F32 = jnp.float32
BF16 = jnp.bfloat16

D_MODEL = 2048
DEPTH = 4
GRID_W = 64
RET_CHUNK = 128
EPS = 1e-5
RW_LN_EPS = 64e-5
RET_HEADS = 4
RET_DK = 128
RET_DV = 128
ATT_HEADS = 8
ATT_KV_HEADS = 2
ATT_HD = 64
ROPE_THETA = 10000.0
HY_C = 512
HY_BANDS = 8
HY_FEAT = 1 + 2 * HY_BANDS
HY_HID = 64
RW_HEADS = 8
RW_N = 64
RW_C = RW_HEADS * RW_N
RW_CHUNK = 64
N_BRANCH = 4
BRANCH_W = 512
MOE_GROUPS = 4
MOE_PER_GROUP = 8
MOE_EXPERTS = MOE_GROUPS * MOE_PER_GROUP
MOE_HID = 256
N_MOD = 6
ALPHA = (2.0 * DEPTH) ** 0.25

MIX_W = 6144
COL_TILE = 256
LANES = 128
VMEM_LIMIT = 56 * 1024 * 1024

T_RQ, T_RK, T_RV, T_RG = 0, 2, 4, 6
T_AQ, T_AKV = 8, 10
T_HY = 11
T_RW = 17


def _cparams(sem):
    return pltpu.CompilerParams(dimension_semantics=sem, vmem_limit_bytes=VMEM_LIMIT)


def _dot(a, b):
    return jnp.dot(a.astype(BF16), b.astype(BF16), preferred_element_type=F32)


def _dot_nt(a, b):
    return lax.dot_general(a.astype(BF16), b.astype(BF16), (((1,), (1,)), ((), ())), preferred_element_type=F32)


def _dot_tn(a, b):
    return lax.dot_general(a.astype(BF16), b.astype(BF16), (((0,), (0,)), ((), ())), preferred_element_type=F32)


def _split(x):
    hi = x.astype(BF16)
    lo = (x - hi.astype(F32)).astype(BF16)
    return hi, lo


def _dot_x2(x, m):
    hi, lo = _split(x)
    return jnp.dot(hi, m, preferred_element_type=F32) + jnp.dot(lo, m, preferred_element_type=F32)


def _dot_m2x(m, x):
    hi, lo = _split(x)
    return jnp.dot(m, hi, preferred_element_type=F32) + jnp.dot(m, lo, preferred_element_type=F32)


def _dot3(x, y):
    xh, xl = _split(x)
    yh, yl = _split(y)
    return (jnp.dot(xh, yh, preferred_element_type=F32) + jnp.dot(xl, yh, preferred_element_type=F32)
            + jnp.dot(xh, yl, preferred_element_type=F32))


def _ln(x, eps=EPS):
    mu = jnp.mean(x, axis=-1, keepdims=True)
    xc = x - mu
    var = jnp.mean(xc * xc, axis=-1, keepdims=True)
    return xc * lax.rsqrt(var + eps)


def _silu(x):
    return x * jax.nn.sigmoid(x)


MOD_TN = 1024


def _mod_kernel(c_ref, w_ref, b_ref, o_ref):
    s = _silu(c_ref[...])
    o_ref[...] = _dot(s, w_ref[...]) + b_ref[...]


def _modulation(cvec, mod_w, mod_b):
    per = D_MODEL // MOD_TN
    return pl.pallas_call(
        _mod_kernel,
        grid=(DEPTH, N_MOD * per),
        in_specs=[
            pl.BlockSpec((8, D_MODEL), lambda l, n: (0, 0)),
            pl.BlockSpec((None, D_MODEL, MOD_TN), lambda l, n: (l, 0, n)),
            pl.BlockSpec((None, 1, MOD_TN), lambda l, n: (l, 0, n)),
        ],
        out_specs=pl.BlockSpec((None, None, 8, MOD_TN), lambda l, n: (l, n // per, 0, n % per)),
        out_shape=jax.ShapeDtypeStruct((DEPTH, N_MOD, 8, D_MODEL), F32),
        compiler_params=_cparams(("arbitrary", "arbitrary")),
        name="modulation",
    )(cvec, mod_w, mod_b.reshape(DEPTH, 1, N_MOD * D_MODEL))


INP_TM = 1024
INP_TN = 512


def _inproj_kernel(x_ref, mod_ref, w_ref, p_ref, h_ref, *, req_base, rows_per_req):
    i = pl.program_id(0)
    j = pl.program_id(1)

    @pl.when(j == 0)
    def _():
        m = req_base + (i * INP_TM) // rows_per_req
        sh = mod_ref[0, pl.ds(m, 1), :]
        sc = mod_ref[1, pl.ds(m, 1), :]
        h_ref[...] = (_ln(x_ref[...]) * (1.0 + sc) + sh).astype(BF16)

    p_ref[...] = jnp.dot(h_ref[...], w_ref[...].astype(BF16), preferred_element_type=F32)


def _inproj(x2d, mod_l, w_in, layer, req_base, rows_per_req):
    n = x2d.shape[0]
    kern = functools.partial(_inproj_kernel, req_base=req_base, rows_per_req=rows_per_req)
    return pl.pallas_call(
        kern,
        grid=(n // INP_TM, MIX_W // INP_TN),
        in_specs=[
            pl.BlockSpec((INP_TM, D_MODEL), lambda i, j: (i, 0)),
            pl.BlockSpec((N_MOD, 8, D_MODEL), lambda i, j: (0, 0, 0)),
            pl.BlockSpec((None, D_MODEL, INP_TN), lambda i, j: (layer, 0, j)),
        ],
        out_specs=[
            pl.BlockSpec((INP_TM, INP_TN), lambda i, j: (i, j)),
            pl.BlockSpec((INP_TM, D_MODEL), lambda i, j: (i, 0)),
        ],
        out_shape=[jax.ShapeDtypeStruct((n, MIX_W), F32), jax.ShapeDtypeStruct((n, D_MODEL), BF16)],
        compiler_params=_cparams(("arbitrary", "arbitrary")),
        name="in_projection",
    )(x2d, mod_l, w_in)


def _retention_kernel(*refs, seq, has_s0, emit_state):
    q_ref, k_ref, v_ref, g_ref, dec_ref = refs[:5]
    pos = 5
    s0_ref = None
    if has_s0:
        s0_ref = refs[pos]
        pos += 1
    o_ref = refs[pos]
    pos += 1
    st_ref = None
    if emit_state:
        st_ref = refs[pos]
        pos += 1
    s_scr, o_scr = refs[pos], refs[pos + 1]

    C = RET_CHUNK
    nc = seq // C
    lg_all = jnp.log1p(-jnp.exp2(-dec_ref[...]))
    ii = lax.broadcasted_iota(jnp.int32, (C, C), 0)
    jj = lax.broadcasted_iota(jnp.int32, (C, C), 1)
    rel = (ii - jj).astype(F32)
    icol = lax.broadcasted_iota(jnp.int32, (C, 1), 0).astype(F32)

    for d in range(2):
        for h in range(RET_HEADS):
            if has_s0:
                s_scr[d, h] = s0_ref[d, h]
            else:
                s_scr[d, h] = jnp.zeros((RET_DK, RET_DV), F32)

    consts = {}
    for d in range(2):
        for h in range(RET_HEADS):
            lg = lg_all[d:d + 1, h * RET_DV:(h + 1) * RET_DV]
            lg1 = lg[:, :1]
            if d == 0:
                dmask = jnp.where(rel >= 0, jnp.exp(lg * rel), 0.0)
                q_dec = jnp.exp(lg1 * (icol + 1.0))
                k_dec = jnp.exp(lg1 * (C - 1.0 - icol))
            else:
                dmask = jnp.where(rel <= 0, jnp.exp(lg * (-rel)), 0.0)
                q_dec = jnp.exp(lg1 * (C - icol))
                k_dec = jnp.exp(lg1 * icol)
            consts[d, h] = (dmask, q_dec, k_dec, jnp.exp(lg * float(C)))

    o_scr[...] = jnp.zeros_like(o_scr)

    def body(ci, carry):
        units = []
        for d in range(2):
            c = ci if d == 0 else nc - 1 - ci
            rows = pl.ds(pl.multiple_of(c * C, C), C)
            for h in range(RET_HEADS):
                cs = slice(h * RET_DK, (h + 1) * RET_DK)
                kc = k_ref[rows, cs] * (RET_DK ** -0.5)
                units.append(dict(d=d, h=h, rows=rows, cs=cs, q16=q_ref[rows, cs].astype(BF16), kc=kc,
                                  v16=v_ref[rows, cs].astype(BF16), s=s_scr[d, h]))
        scores = [_dot_nt(u['q16'], u['kc']) for u in units]
        cross = [_dot(u['q16'], u['s']) for u in units]
        kv = [_dot_tn(u['kc'] * consts[u['d'], u['h']][2], u['v16']) for u in units]
        inner = [_dot(sc * consts[u['d'], u['h']][0], u['v16']) for sc, u in zip(scores, units)]
        for u, cr, upd, inn in zip(units, cross, kv, inner):
            _, q_dec, _, c_dec = consts[u['d'], u['h']]
            s_scr[u['d'], u['h']] = u['s'] * c_dec + upd
            o_scr[u['rows'], u['cs']] = o_scr[u['rows'], u['cs']] + (inn + cr * q_dec)
        return carry

    lax.fori_loop(0, nc, body, 0)

    for h in range(RET_HEADS):
        cs = slice(h * RET_DV, (h + 1) * RET_DV)
        o_ref[:, cs] = (_ln(o_scr[:, cs]) * _silu(g_ref[:, cs])).astype(BF16)
    if emit_state:
        st_ref[...] = s_scr[...]


def _retention(p, dec_rep, s0, batch, seq, emit_state):
    n = batch * seq
    w = RET_HEADS * RET_DK
    has_s0 = s0 is not None
    kern = functools.partial(_retention_kernel, seq=seq, has_s0=has_s0, emit_state=emit_state)
    in_specs = [
        pl.BlockSpec((seq, w), lambda b: (b, 0)),
        pl.BlockSpec((seq, w), lambda b: (b, 1)),
        pl.BlockSpec((seq, w), lambda b: (b, 2)),
        pl.BlockSpec((seq, w), lambda b: (b, 3)),
        pl.BlockSpec((2, w), lambda b: (0, 0)),
    ]
    args = [p, p, p, p, dec_rep]
    if has_s0:
        in_specs.append(pl.BlockSpec((None, 2, RET_HEADS, RET_DK, RET_DV), lambda b: (b, 0, 0, 0, 0)))
        args.append(s0)
    out_specs = [pl.BlockSpec((seq, w), lambda b: (b, 0))]
    out_shape = [jax.ShapeDtypeStruct((n, w), BF16)]
    if emit_state:
        out_specs.append(pl.BlockSpec((None, 2, RET_HEADS, RET_DK, RET_DV), lambda b: (b, 0, 0, 0, 0)))
        out_shape.append(jax.ShapeDtypeStruct((batch, 2, RET_HEADS, RET_DK, RET_DV), F32))
    return pl.pallas_call(
        kern,
        grid=(batch,),
        in_specs=in_specs,
        out_specs=out_specs,
        out_shape=out_shape,
        scratch_shapes=[pltpu.VMEM((2, RET_HEADS, RET_DK, RET_DV), F32), pltpu.VMEM((seq, w), F32)],
        compiler_params=_cparams(("arbitrary",)),
        name="retention",
    )(*args)


ATT_QB = 256


def _head_rms(x, ones_bd, gain):
    ss = _dot_x2(x * x, ones_bd)
    return x * lax.rsqrt(ss * (1.0 / ATT_HD) + EPS) * gain


def _rope(x, cos, sin_signed):
    w = x.shape[-1]
    lane = lax.broadcasted_iota(jnp.int32, x.shape, 1)
    first = (lane & 16) == 0
    swapped = jnp.where(first, pltpu.roll(x, w - 16, axis=1), pltpu.roll(x, 16, axis=1))
    return x * cos + swapped * sin_signed


def _attention_kernel(*refs, seq, decode, past):
    q_ref, kv_ref, gq_ref, gk_ref, bdq_ref, bdk_ref = refs[:6]
    pos = 6
    if decode:
        ck_ref, cv_ref, cos_ref, sin_ref = refs[pos:pos + 4]
        pos += 4
    o_ref = refs[pos]
    pos += 1
    if not decode:
        ko_ref, vo_ref = refs[pos:pos + 2]
        pos += 2
    q_scr, k_scr, v_scr = refs[pos:pos + 3]

    kvw = ATT_KV_HEADS * ATT_HD
    q = _head_rms(q_ref[...], bdq_ref[...], gq_ref[...])
    kv = kv_ref[...]
    k = _head_rms(kv[:, :kvw], bdk_ref[...], gk_ref[...])
    v = kv[:, kvw:]
    if decode:
        cos = cos_ref[...]
        sin = sin_ref[...]
        q = _rope(q, cos, sin)
        k = _rope(k, cos[:, :kvw], sin[:, :kvw])
        k_scr[0:past, :] = ck_ref[...].astype(BF16)
        v_scr[0:past, :] = cv_ref[...].astype(BF16)
        k_scr[past:past + seq, :] = k.astype(BF16)
        v_scr[past:past + seq, :] = v.astype(BF16)
    else:
        ko_ref[...] = k
        vo_ref[...] = v
        k_scr[...] = k.astype(BF16)
        v_scr[...] = v.astype(BF16)
    q_scr[...] = (q * (ATT_HD ** -0.5)).astype(BF16)

    grp = ATT_HEADS // ATT_KV_HEADS

    def body(qi, carry):
        r0 = pl.multiple_of(qi * ATT_QB, ATT_QB)
        outs = []
        for h in range(ATT_HEADS):
            g = h // grp
            qh = q_scr[pl.ds(r0, ATT_QB), h * ATT_HD:(h + 1) * ATT_HD]
            kh = k_scr[:, g * ATT_HD:(g + 1) * ATT_HD]
            vh = v_scr[:, g * ATT_HD:(g + 1) * ATT_HD]
            s = lax.dot_general(qh, kh, (((1,), (1,)), ((), ())), preferred_element_type=F32)
            s = s - jnp.max(s, axis=-1, keepdims=True)
            e = jnp.exp(s)
            prob = e / jnp.sum(e, axis=-1, keepdims=True)
            outs.append(jnp.dot(prob.astype(BF16), vh, preferred_element_type=F32))
        o_ref[pl.ds(r0, ATT_QB), :] = jnp.concatenate(outs, axis=1).astype(BF16)
        return carry

    lax.fori_loop(0, seq // ATT_QB, body, 0)


def _attention(p, gq, gk, bdq, bdk, batch, seq, cache=None):
    n = batch * seq
    qw = ATT_HEADS * ATT_HD
    kvw = ATT_KV_HEADS * ATT_HD
    decode = cache is not None
    past = cache[0].shape[1] if decode else 0
    kern = functools.partial(_attention_kernel, seq=seq, decode=decode, past=past)
    in_specs = [
        pl.BlockSpec((seq, qw), lambda b: (b, T_AQ * COL_TILE // qw)),
        pl.BlockSpec((seq, 2 * kvw), lambda b: (b, T_AKV * COL_TILE // (2 * kvw))),
        pl.BlockSpec((1, qw), lambda b: (0, 0)),
        pl.BlockSpec((1, kvw), lambda b: (0, 0)),
        pl.BlockSpec((qw, qw), lambda b: (0, 0)),
        pl.BlockSpec((kvw, kvw), lambda b: (0, 0)),
    ]
    args = [p, p, gq, gk, bdq, bdk]
    out_specs = [pl.BlockSpec((seq, qw), lambda b: (b, 0))]
    out_shape = [jax.ShapeDtypeStruct((n, qw), BF16)]
    if decode:
        ck, cv, cos, sin = cache
        in_specs += [
            pl.BlockSpec((None, past, kvw), lambda b: (b, 0, 0)),
            pl.BlockSpec((None, past, kvw), lambda b: (b, 0, 0)),
            pl.BlockSpec((seq, qw), lambda b: (0, 0)),
            pl.BlockSpec((seq, qw), lambda b: (0, 0)),
        ]
        args += [ck, cv, cos, sin]
    else:
        out_specs += [pl.BlockSpec((None, seq, kvw), lambda b: (b, 0, 0))] * 2
        out_shape += [jax.ShapeDtypeStruct((batch, seq, kvw), F32)] * 2
    return pl.pallas_call(
        kern,
        grid=(batch,),
        in_specs=in_specs,
        out_specs=out_specs,
        out_shape=out_shape,
        scratch_shapes=[pltpu.VMEM((seq, qw), BF16), pltpu.VMEM((past + seq, kvw), BF16),
                        pltpu.VMEM((past + seq, kvw), BF16)],
        compiler_params=_cparams(("arbitrary",)),
        name="attention",
    )(*args)


def _rope_tables(seq):
    t = jnp.arange(seq, dtype=jnp.int32)
    row = (t // GRID_W).astype(F32)
    col = (t % GRID_W).astype(F32)
    nf = ATT_HD // 4
    inv = ROPE_THETA ** (-jnp.arange(nf, dtype=F32) / nf)
    a_row = row[:, None] * inv[None, :]
    a_col = col[:, None] * inv[None, :]
    ang = jnp.concatenate([a_row, a_row, a_col, a_col], axis=1)
    sign = jnp.concatenate([-jnp.ones((nf,), F32), jnp.ones((nf,), F32)] * 2)
    cos = jnp.tile(jnp.cos(ang), (1, ATT_HEADS))
    sin = jnp.tile(jnp.sin(ang) * sign[None, :], (1, ATT_HEADS))
    return cos, sin


def _dft_matrices(seq):
    two_l = 2 * seq
    f = jnp.arange(seq, dtype=jnp.int32)
    t = jnp.arange(seq, dtype=jnp.int32)
    m_fwd = (f[:, None] * t[None, :]) % two_l
    ang_fwd = m_fwd.astype(F32) * (math.pi / seq)
    alt_t = jnp.where(t % 2 == 0, 1.0, -1.0).astype(F32)
    fc = jnp.cos(ang_fwd)
    fs = jnp.where(f[:, None] == 0, alt_t[None, :], jnp.sin(ang_fwd))
    n_out = t + seq // 2
    m_inv = (n_out[:, None] * f[None, :]) % two_l
    ang_inv = m_inv.astype(F32) * (math.pi / seq)
    wgt = jnp.where(f == 0, 1.0, 2.0).astype(F32) / two_l
    alt_n = jnp.where(n_out % 2 == 0, 1.0, -1.0).astype(F32)
    gc = jnp.cos(ang_inv) * wgt[None, :]
    gs = jnp.where(f[None, :] == 0, alt_n[:, None] / two_l, jnp.sin(ang_inv) * wgt[None, :])
    return tuple(m.astype(BF16) for m in (fc, fs, gc, gs))


def _hyena_feats(seq):
    t = jnp.arange(seq, dtype=F32) / seq
    bands = jnp.arange(1, HY_BANDS + 1, dtype=F32)
    ang = 2.0 * jnp.pi * t[:, None] * bands[None, :]
    z = jnp.concatenate([t[:, None], jnp.sin(ang), jnp.cos(ang)], -1)
    return jnp.pad(z, ((0, 0), (0, LANES - HY_FEAT)))


def _hyena_filter_kernel(z_ref, w1_ref, b1_ref, w2_ref, b2_ref, fr_ref, w3_ref, lr_ref,
                         fc_ref, fs_ref, hc_ref, hs_ref, *, seq):
    fr = fr_ref[...]
    h = jnp.sin(fr[0:1, :] * (_dot3(z_ref[...], w1_ref[...]) + b1_ref[...]))
    h = jnp.sin(fr[1:2, :] * (_dot3(h, w2_ref[...]) + b2_ref[...]))
    h = _dot3(h, w3_ref[...])
    t = lax.broadcasted_iota(jnp.int32, (seq, 1), 0).astype(F32)
    dist = jnp.abs(t - float(seq // 2)) / (0.5 * seq)
    h = h * jnp.exp(-jnp.exp(lr_ref[...]) * dist)
    h16 = h.astype(BF16)
    hc_ref[...] = jnp.dot(fc_ref[...], h16, preferred_element_type=F32)
    hs_ref[...] = jnp.dot(fs_ref[...], h16, preferred_element_type=F32)


def _hyena_filters(z, w1p, b1, w2, b2, fr, w3, lr, dft, seq):
    fc, fs = dft[:2]
    full = lambda a: pl.BlockSpec(a.shape, lambda: (0,) * a.ndim)
    args = [z, w1p, b1, w2, b2, fr, w3, lr, fc, fs]
    return pl.pallas_call(
        functools.partial(_hyena_filter_kernel, seq=seq),
        in_specs=[full(a) for a in args],
        out_specs=[pl.BlockSpec((seq, 2 * HY_C), lambda: (0, 0))] * 2,
        out_shape=[jax.ShapeDtypeStruct((seq, 2 * HY_C), F32)] * 2,
        compiler_params=pltpu.CompilerParams(vmem_limit_bytes=VMEM_LIMIT),
        name="hyena_filters",
    )(*args)


def _shift_rows(u, seq):
    row = lax.broadcasted_iota(jnp.int32, u.shape, 0)
    prev = jnp.where(row == 0, 0.0, pltpu.roll(u, 1, axis=0))
    nxt = jnp.where(row == seq - 1, 0.0, pltpu.roll(u, seq - 1, axis=0))
    return prev, nxt


def _hyena_kernel(v_ref, x1_ref, x2_ref, cv_ref, c1_ref, c2_ref, bias_ref,
                  hc0_ref, hs0_ref, hc1_ref, hs1_ref,
                  fc_ref, fs_ref, gc_ref, gs_ref, o_ref, *, seq):
    def dwconv(u_ref, w_ref):
        u = u_ref[...]
        w = w_ref[...]
        prev, nxt = _shift_rows(u, seq)
        return prev * w[0:1, :] + u * w[1:2, :] + nxt * w[2:3, :]

    row0 = lax.broadcasted_iota(jnp.int32, (seq, COL_TILE), 0) == 0

    def long_conv(u, hc, hs):
        u16 = u.astype(BF16)
        uc = jnp.dot(fc_ref[...], u16, preferred_element_type=F32)
        us = jnp.dot(fs_ref[...], u16, preferred_element_type=F32)
        ss = us * hs
        yc = uc * hc - jnp.where(row0, 0.0, ss)
        ys = jnp.where(row0, ss, uc * hs + us * hc)
        return (jnp.dot(gc_ref[...], yc.astype(BF16), preferred_element_type=F32)
                + jnp.dot(gs_ref[...], ys.astype(BF16), preferred_element_type=F32))

    bias = bias_ref[...]
    v = dwconv(v_ref, cv_ref)
    z = long_conv(v, hc0_ref[...], hs0_ref[...]) + bias[0:1, :] * v
    z = dwconv(x1_ref, c1_ref) * z
    z = long_conv(z, hc1_ref[...], hs1_ref[...]) + bias[1:2, :] * z
    o_ref[...] = (dwconv(x2_ref, c2_ref) * z).astype(BF16)


def _hyena(p, hy_conv, hy_bias, hc, hs, dft, layer, batch, seq):
    n = batch * seq
    nb = HY_C // COL_TILE
    tile = lambda off: pl.BlockSpec((seq, COL_TILE), lambda b, c: (b, T_HY + off * nb + c))
    cw = lambda off: pl.BlockSpec((None, 3, COL_TILE), lambda b, c: (layer, 0, off * nb + c))
    filt = lambda o: pl.BlockSpec((seq, COL_TILE), lambda b, c: (0, o * nb + c))
    mat = pl.BlockSpec((seq, seq), lambda b, c: (0, 0))
    return pl.pallas_call(
        functools.partial(_hyena_kernel, seq=seq),
        grid=(batch, nb),
        in_specs=[tile(0), tile(1), tile(2), cw(0), cw(1), cw(2),
                  pl.BlockSpec((None, 2, COL_TILE), lambda b, c: (layer, 0, c)),
                  filt(0), filt(0), filt(1), filt(1)] + [mat] * 4,
        out_specs=pl.BlockSpec((seq, COL_TILE), lambda b, c: (b, c)),
        out_shape=jax.ShapeDtypeStruct((n, HY_C), BF16),
        compiler_params=_cparams(("arbitrary", "arbitrary")),
        name="hyena",
    )(p, p, p, hy_conv, hy_conv, hy_conv, hy_bias, hc, hs, hc, hs, *dft)


def _stack2(x):
    lane = lax.broadcasted_iota(jnp.int32, x.shape, 1)
    first = lane < RW_N
    return jnp.concatenate([jnp.where(first, x, 0.0), jnp.where(first, 0.0, x)], axis=0)


def _rwkv_kernel(*refs, seq, has_s0):
    (pr_ref, pk_ref, pv_ref, pl_ref, mur_ref, muk_ref, muv_ref, mul_ref, aup_ref, gup_ref, wup_ref, w0_ref,
     a0_ref, kk_ref, ka_ref, rk_ref, bd_ref) = refs[:17]
    pos = 17
    s0_ref = None
    if has_s0:
        s0_ref = refs[pos]
        pos += 1
    o_ref, st_ref = refs[pos], refs[pos + 1]
    r_s, k_s, v_s, a_s, b_s, gate_s, bonus_s, y_scr, e_s, pre16, pre32 = refs[pos + 2:pos + 13]

    def shifted(ref, mu_ref):
        p = ref[...]
        prev, nxt = _shift_rows(p, seq)
        return p + mu_ref[...] * (0.5 * (prev + nxt) - p)

    bd = bd_ref[...]
    r = shifted(pr_ref, mur_ref)
    k = shifted(pk_ref, muk_ref)
    v = shifted(pv_ref, muv_ref)
    low = shifted(pl_ref, mul_ref)
    wa = low[:, :LANES]
    iclr = jax.nn.sigmoid(a0_ref[...] + _dot(wa, aup_ref[...]))
    gate_s[...] = _dot(jax.nn.sigmoid(low[:, LANES:]), gup_ref[...])
    kk = k * kk_ref[...]
    kk = kk * lax.rsqrt(_dot_x2(kk * kk, bd) + 1e-12)
    k2 = k * (1.0 + (iclr - 1.0) * ka_ref[...])
    bonus_s[...] = _dot_x2(r * k2 * rk_ref[...], bd) * v
    r_s[...] = r
    k_s[...] = k2
    v_s[...] = v
    a_s[...] = -kk
    b_s[...] = kk * iclr
    tw = jnp.tanh(wa)
    for d in range(2):
        x = -(w0_ref[d:d + 1, :] + _dot(tw, wup_ref[d]))
        softplus = jnp.maximum(x, 0.0) + jnp.log1p(jnp.exp(-jnp.abs(x)))
        e_s[d] = jnp.exp(-softplus - 0.5)

    C = RW_CHUNK
    S = 2 * C
    nc = seq // C
    ri = lax.broadcasted_iota(jnp.int32, (S, S), 0)
    ci = lax.broadcasted_iota(jnp.int32, (S, S), 1)
    eye = ri == ci
    eye_f = jnp.where(eye, 1.0, 0.0)
    tr = ri & (C - 1)
    tc = ci & (C - 1)
    ti = lax.broadcasted_iota(jnp.int32, (C, C), 0)
    tj = lax.broadcasted_iota(jnp.int32, (C, C), 1)
    dir_consts = (
        (tc < tr, tc <= tr, (tj <= ti).astype(BF16), C - 1),
        (tc > tr, tc >= tr, (tj >= ti).astype(BF16), 0),
    )
    level_masks = []
    m = 1
    while m < C:
        lg = m.bit_length() - 1
        same = jnp.right_shift(tr, lg + 1) == jnp.right_shift(tc, lg + 1)
        halves = (jnp.right_shift(tr, lg) & 1) != (jnp.right_shift(tc, lg) & 1)
        level_masks.append(same & halves)
        m *= 2

    def bf(x):
        return x.astype(BF16)

    def mm(x, y):
        return jnp.dot(x, y, preferred_element_type=F32)

    def mm_nt(x, y):
        return lax.dot_general(x, y, (((1,), (1,)), ((), ())), preferred_element_type=F32)

    def mm_tn(x, y):
        return lax.dot_general(x, y, (((0,), (0,)), ((), ())), preferred_element_type=F32)

    def phase1(units):
        fr = []
        row_sl = [pl.ds(pl.multiple_of(c * C, C), C) for _, c in units]
        ecs = [e_s[d, rows, :] for (d, _), rows in zip(units, row_sl)]
        cums = [_dot_m2x(dir_consts[d][2], ec) for (d, _), ec in zip(units, ecs)]
        for (d, c), rows, ec, cum in zip(units, row_sl, ecs, cums):
            strict, incl, tri, last = dir_consts[d]
            g_c = cum[last:last + 1, :]
            inv = jnp.exp(cum)
            to_end = jnp.exp(cum - g_c)
            a16 = bf(_stack2(a_s[rows, :] * jnp.exp(ec - cum)))
            r_t = _stack2(r_s[rows, :] * jnp.exp(-cum))
            bc = b_s[rows, :]
            kc = k_s[rows, :]
            gm = mm_nt(jnp.concatenate([a16, bf(r_t)], axis=0),
                       jnp.concatenate([bf(_stack2(bc * inv)), bf(_stack2(kc * inv))], axis=0))
            fr.append(dict(
                d=d, c=c, g_c=g_c, a16=a16, r_t=r_t,
                bh16=bf(_stack2(bc * to_end)), kh16=bf(_stack2(kc * to_end)), v16=bf(_stack2(v_s[rows, :])),
                n_m=jnp.where(strict, gm[:S, :S], 0.0),
                ak16=bf(jnp.where(strict, gm[:S, S:], 0.0)),
                rb16=bf(jnp.where(incl, gm[S:, :S], 0.0)),
                rk16=bf(jnp.where(incl, gm[S:, S:], 0.0))))
        nu = len(fr)
        ts = [None] * nu
        for li, mask in enumerate(level_masks):
            if li == 0:
                ts = [eye_f + jnp.where(mask, f['n_m'], 0.0) for f in fr]
                continue
            t16 = [bf(t) for t in ts]
            inner = [mm(bf(jnp.where(mask, fr[u]['n_m'], 0.0)), t16[u]) for u in range(nu)]
            ts = [ts[u] + mm(t16[u], bf(inner[u])) for u in range(nu)]
        akv = [mm(f['ak16'], f['v16']) for f in fr]
        wu = [mm(bf(ts[u]), jnp.concatenate([fr[u]['a16'], bf(akv[u])], axis=1)) for u in range(nu)]
        w16 = [bf(x[:, :S]) for x in wu]
        uv16 = [jnp.concatenate([bf(wu[u][:, S:]), fr[u]['v16']], axis=0) for u in range(nu)]
        for u, f in enumerate(fr):
            d, c = f['d'], f['c']
            pre16[d, c, 0] = bf(f['r_t'] + mm(f['rb16'], w16[u]))
            pre32[d, c, 0] = mm(jnp.concatenate([f['rb16'], f['rk16']], axis=1), uv16[u])
            pre16[d, c, 1] = bf(jnp.where(eye, jnp.exp(-f['g_c']), 0.0) + mm_tn(w16[u], f['bh16']))
            pre32[d, c, 1] = mm_tn(uv16[u], jnp.concatenate([f['bh16'], f['kh16']], axis=0))

    group = min(nc, 4)
    if nc == group:
        phase1([(d, c) for d in range(2) for c in range(nc)])
    else:
        def phase1_body(i, carry):
            phase1([(d, i * group + j) for d in range(2) for j in range(group)])
            return carry

        lax.fori_loop(0, nc // group, phase1_body, 0)

    for d in range(2):
        st_ref[d] = s0_ref[d] if has_s0 else jnp.zeros((S, S), F32)
    y_scr[...] = jnp.zeros_like(y_scr)

    def phase2(i):
        cs = (i, nc - 1 - i)
        s16 = [bf(st_ref[d]) for d in range(2)]
        for d in range(2):
            st_ref[d] = mm(s16[d], pre16[d, cs[d], 1]) + pre32[d, cs[d], 1]
        for d in range(2):
            rows = pl.ds(pl.multiple_of(cs[d] * C, C), C)
            y_st = mm_nt(pre16[d, cs[d], 0], s16[d]) + pre32[d, cs[d], 0]
            y_scr[rows, :] = y_scr[rows, :] + (y_st[:C, :] + y_st[C:, :])

    if nc == group:
        for i in range(nc):
            phase2(i)
    else:
        def phase2_body(i, carry):
            phase2(i)
            return carry

        lax.fori_loop(0, nc, phase2_body, 0)

    y = y_scr[...]
    ones_bd = bd_ref[...]
    mean = _dot_x2(y, ones_bd) * (1.0 / RW_N)
    yc = y - mean
    var = _dot_x2(yc * yc, ones_bd) * (1.0 / RW_N)
    yn = yc * lax.rsqrt(var + RW_LN_EPS)
    o_ref[...] = ((yn + bonus_s[...]) * gate_s[...]).astype(BF16)


def _rwkv(p, mu, aup, gup, wup, w0, a0, kk, ka, rk, bd2, s0, batch, seq):
    n = batch * seq
    npair = RW_HEADS // 2
    has_s0 = s0 is not None
    S = 2 * RW_CHUNK
    base = T_RW * COL_TILE // LANES
    per = RW_C // LANES
    pcol = lambda sec: pl.BlockSpec((seq, LANES), lambda b, h: (b, base + sec * per + h))
    mucol = lambda sec: pl.BlockSpec((1, LANES), lambda b, h: (0, sec * per + h))
    vec = pl.BlockSpec((1, LANES), lambda b, h: (0, h))
    low_idx = 3 * RW_C // COL_TILE
    in_specs = [
        pcol(0), pcol(1), pcol(2),
        pl.BlockSpec((seq, COL_TILE), lambda b, h: (b, T_RW + low_idx)),
        mucol(0), mucol(1), mucol(2),
        pl.BlockSpec((1, COL_TILE), lambda b, h: (0, low_idx)),
        pl.BlockSpec((LANES, LANES), lambda b, h: (0, h)),
        pl.BlockSpec((LANES, LANES), lambda b, h: (0, h)),
        pl.BlockSpec((2, LANES, LANES), lambda b, h: (0, 0, h)),
        pl.BlockSpec((2, LANES), lambda b, h: (0, h)),
        vec, vec, vec, vec,
        pl.BlockSpec((LANES, LANES), lambda b, h: (0, 0)),
    ]
    args = [p, p, p, p, mu, mu, mu, mu, aup, gup, wup, w0, a0, kk, ka, rk, bd2]
    if has_s0:
        in_specs.append(pl.BlockSpec((None, 2, None, S, S), lambda b, h: (b, 0, h, 0, 0)))
        args.append(s0)
    seq_buf = pltpu.VMEM((seq, LANES), F32)
    return pl.pallas_call(
        functools.partial(_rwkv_kernel, seq=seq, has_s0=has_s0),
        grid=(batch, npair),
        in_specs=in_specs,
        out_specs=[pl.BlockSpec((seq, LANES), lambda b, h: (b, h)),
                   pl.BlockSpec((None, 2, None, S, S), lambda b, h: (b, 0, h, 0, 0))],
        out_shape=[jax.ShapeDtypeStruct((n, RW_C), BF16), jax.ShapeDtypeStruct((batch, 2, npair, S, S), F32)],
        scratch_shapes=[seq_buf] * 8 + [pltpu.VMEM((2, seq, LANES), F32),
                                        pltpu.VMEM((2, seq // RW_CHUNK, 2, S, S), BF16),
                                        pltpu.VMEM((2, seq // RW_CHUNK, 2, S, S), F32)],
        compiler_params=_cparams(("arbitrary", "arbitrary")),
        name="rwkv7",
    )(*args)


def _pair_states(s):
    b = s.shape[0]
    s = s.reshape(b, 2, RW_HEADS // 2, 2, RW_N, RW_N)
    z = jnp.zeros_like(s[:, :, :, 0])
    top = jnp.concatenate([s[:, :, :, 0], z], axis=-1)
    bot = jnp.concatenate([z, s[:, :, :, 1]], axis=-1)
    return jnp.concatenate([top, bot], axis=-2)


def _unpair_states(s):
    b = s.shape[0]
    h0 = s[:, :, :, :RW_N, :RW_N]
    h1 = s[:, :, :, RW_N:, RW_N:]
    return jnp.stack([h0, h1], axis=3).reshape(b, 2, RW_HEADS, RW_N, RW_N)


MRG_TM = 1024
MRG_TN = 256


def _merge_kernel(h_ref, ba_ref, bb_ref, bc_ref, bd_ref, g0_ref, g1_ref, g2_ref, g3_ref, wb_ref, o_ref):
    h = h_ref[...]
    acc = None
    for i, (br, gw) in enumerate(((ba_ref, g0_ref), (bb_ref, g1_ref), (bc_ref, g2_ref), (bd_ref, g3_ref))):
        gate = jax.nn.sigmoid(jnp.dot(h, gw[...].astype(BF16), preferred_element_type=F32))
        proj = jnp.dot(br[...], wb_ref[i].astype(BF16), preferred_element_type=F32)
        acc = gate * proj if acc is None else acc + gate * proj
    o_ref[...] = acc.astype(BF16)


def _merge(h, branches, w_in, w_branch, layer):
    n = h.shape[0]
    gate_spec = lambda b: pl.BlockSpec(
        (None, D_MODEL, MRG_TN), lambda i, j: (layer, 0, (MIX_W + b * D_MODEL) // MRG_TN + j))
    return pl.pallas_call(
        _merge_kernel,
        grid=(n // MRG_TM, D_MODEL // MRG_TN),
        in_specs=[pl.BlockSpec((MRG_TM, D_MODEL), lambda i, j: (i, 0))]
        + [pl.BlockSpec((MRG_TM, BRANCH_W), lambda i, j: (i, 0))] * N_BRANCH
        + [gate_spec(b) for b in range(N_BRANCH)]
        + [pl.BlockSpec((None, N_BRANCH, BRANCH_W, MRG_TN), lambda i, j: (layer, 0, 0, j))],
        out_specs=pl.BlockSpec((MRG_TM, MRG_TN), lambda i, j: (i, j)),
        out_shape=jax.ShapeDtypeStruct((n, D_MODEL), BF16),
        compiler_params=_cparams(("arbitrary", "arbitrary")),
        name="gated_merge",
    )(h, *branches, w_in, w_in, w_in, w_in, w_branch)


OUT_TM = 256


def _route(logits):
    lane_i = lax.broadcasted_iota(jnp.int32, logits.shape, 1)
    lane = lane_i.astype(F32)
    grp_of_lane = jnp.right_shift(lane_i, MOE_PER_GROUP.bit_length() - 1).astype(F32)
    neg = -jnp.inf
    is_grp = (lane_i >= MOE_EXPERTS) & (lane_i < MOE_EXPERTS + MOE_GROUPS)
    gl = jnp.where(is_grp, logits, neg)
    gmax = jnp.max(gl, axis=-1, keepdims=True)
    gsel = jnp.min(jnp.where(gl == gmax, lane, float(LANES)), axis=-1, keepdims=True) - float(MOE_EXPERTS)
    gval = 1.0 / jnp.sum(jnp.exp(gl - gmax), axis=-1, keepdims=True)
    in_grp = (grp_of_lane == gsel) & (lane_i < MOE_EXPERTS)
    el = jnp.where(in_grp, logits, neg)
    ee = jnp.exp(el - jnp.max(el, axis=-1, keepdims=True))
    prob = jnp.where(in_grp, ee / jnp.sum(ee, axis=-1, keepdims=True), -1.0)
    v1 = jnp.max(prob, axis=-1, keepdims=True)
    i1 = jnp.min(jnp.where(prob == v1, lane, float(LANES)), axis=-1, keepdims=True)
    prob2 = jnp.where(lane == i1, -1.0, prob)
    v2 = jnp.max(prob2, axis=-1, keepdims=True)
    i2 = jnp.min(jnp.where(prob2 == v2, lane, float(LANES)), axis=-1, keepdims=True)
    tot = v1 + v2
    comb = jnp.where(lane == i1, gval * (v1 / tot), 0.0) + jnp.where(lane == i2, gval * (v2 / tot), 0.0)
    comb = comb + jnp.where(lane_i == LANES - 1, gsel, 0.0)
    return comb, jnp.where(lane == gsel, 1.0, 0.0)


def _outproj_kernel(m_ref, w_ref, x_ref, mod_ref, lng_ref, lnb_ref, rwh_ref, rwl_ref, rb_ref,
                    x1_ref, h2_ref, comb_ref, cnt_ref, *, req_base, rows_per_req):
    i = pl.program_id(0)
    mix = jnp.dot(m_ref[...], w_ref[...], preferred_element_type=F32)
    m = req_base + (i * OUT_TM) // rows_per_req
    g1 = mod_ref[2, pl.ds(m, 1), :]
    sh2 = mod_ref[3, pl.ds(m, 1), :]
    sc2 = mod_ref[4, pl.ds(m, 1), :]
    x1 = _ln(ALPHA * x_ref[...] + g1 * mix) * lng_ref[0:1, :] + lnb_ref[0:1, :]
    x1_ref[...] = x1
    h2 = _ln(x1) * (1.0 + sc2) + sh2
    h2_ref[...] = h2.astype(BF16)
    hh, hl = _split(h2)
    rwh = rwh_ref[...]
    logits = (jnp.dot(hh, rwh, preferred_element_type=F32) + jnp.dot(hl, rwh, preferred_element_type=F32)
              + jnp.dot(hh, rwl_ref[...], preferred_element_type=F32)) + rb_ref[...]
    comb, grp_onehot = _route(logits)
    comb_ref[...] = comb
    cnt_ref[...] = jnp.broadcast_to(jnp.sum(grp_onehot, axis=0, keepdims=True), cnt_ref.shape)


def _outproj(merged, w_out16, x2d, mod_l, ln_g, ln_b, rwh, rwl, rb, layer, req_base, rows_per_req):
    n = x2d.shape[0]
    kern = functools.partial(_outproj_kernel, req_base=req_base, rows_per_req=rows_per_req)
    row = lambda w: pl.BlockSpec((OUT_TM, w), lambda i: (i, 0))
    full = lambda a: pl.BlockSpec(a.shape, lambda i: (0,) * a.ndim)
    lnspec = pl.BlockSpec((None, 2, D_MODEL), lambda i: (layer, 0, 0))
    return pl.pallas_call(
        kern,
        grid=(n // OUT_TM,),
        in_specs=[row(D_MODEL),
                  pl.BlockSpec((None, D_MODEL, D_MODEL), lambda i: (layer, 0, 0)),
                  row(D_MODEL), full(mod_l), lnspec, lnspec, full(rwh), full(rwl), full(rb)],
        out_specs=[row(D_MODEL), row(D_MODEL), row(LANES), pl.BlockSpec((None, 8, LANES), lambda i: (i, 0, 0))],
        out_shape=[jax.ShapeDtypeStruct((n, D_MODEL), F32), jax.ShapeDtypeStruct((n, D_MODEL), BF16),
                   jax.ShapeDtypeStruct((n, LANES), F32), jax.ShapeDtypeStruct((n // OUT_TM, 8, LANES), F32)],
        compiler_params=_cparams(("arbitrary",)),
        name="out_projection",
    )(merged, w_out16, x2d, mod_l, ln_g, ln_b, rwh, rwl, rb)


MOE_TM = 1024
MOE_RB = 256
MOE_CB = 512


def _moe_kernel(offs_ref, h_ref, comb_ref, wg_ref, wu_ref, wd_ref, o_ref, xs_scr, cs_scr, pm_scr, wg16, wu16, wd16):
    i = pl.program_id(0)
    e = pl.program_id(1)

    @pl.when(e == 0)
    def _():
        comb = comb_ref[...]
        lane_f = lax.broadcasted_iota(jnp.int32, comb.shape, 1).astype(F32)
        onehot = jnp.where(lane_f == comb[:, LANES - 1:LANES], 1.0, 0.0)
        r_t = lax.broadcasted_iota(jnp.int32, (MOE_TM, MOE_TM), 0)
        c_t = lax.broadcasted_iota(jnp.int32, (MOE_TM, MOE_TM), 1)
        tri = (c_t <= r_t).astype(BF16)
        rank = jnp.dot(tri, onehot.astype(BF16), preferred_element_type=F32)
        cnt = rank[MOE_TM - 1:MOE_TM, :]
        r_g = lax.broadcasted_iota(jnp.int32, (LANES, LANES), 0)
        c_g = lax.broadcasted_iota(jnp.int32, (LANES, LANES), 1)
        start = _dot_x2(cnt, (r_g < c_g).astype(BF16))
        pos = jnp.sum(onehot * (start + rank - 1.0), axis=-1, keepdims=True)
        pm = jnp.where(c_t.astype(F32) == pos, 1.0, 0.0).astype(BF16)
        pm_scr[...] = pm
        for cb in range(D_MODEL // MOE_CB):
            cols = slice(cb * MOE_CB, (cb + 1) * MOE_CB)
            xs_scr[:, cols] = lax.dot_general(pm, h_ref[:, cols], (((0,), (0,)), ((), ())),
                                              preferred_element_type=F32).astype(BF16)
        c_hi = comb.astype(BF16)
        c_mid = (comb - c_hi.astype(F32)).astype(BF16)
        c_lo = (comb - c_hi.astype(F32) - c_mid.astype(F32)).astype(BF16)
        tn = lambda y: lax.dot_general(pm, y, (((0,), (0,)), ((), ())), preferred_element_type=F32)
        cs_scr[...] = tn(c_hi) + tn(c_mid) + tn(c_lo)
        o_ref[...] = jnp.zeros_like(o_ref)

    g = e // MOE_PER_GROUP
    lo = offs_ref[i, g]
    hi = offs_ref[i, g + 1]

    @pl.when(hi > lo)
    def _():
        wg16[...] = wg_ref[...].astype(BF16)
        wu16[...] = wu_ref[...].astype(BF16)
        wd16[...] = wd_ref[...].astype(BF16)

        def block(j, carry):
            rows = pl.ds(pl.multiple_of(j * MOE_RB, MOE_RB), MOE_RB)
            x = xs_scr[rows, :]
            cs = cs_scr[rows, :]
            lane = lax.broadcasted_iota(jnp.int32, cs.shape, 1)
            ce = jnp.sum(jnp.where(lane == e, cs, 0.0), axis=-1, keepdims=True)
            hg = jnp.dot(x, wg16[...], preferred_element_type=F32)
            hu = jnp.dot(x, wu16[...], preferred_element_type=F32)
            act = _silu(hg) * hu * ce
            o_ref[rows, :] += jnp.dot(act.astype(BF16), wd16[...], preferred_element_type=F32)
            return carry

        lax.fori_loop(lo // MOE_RB, (hi + MOE_RB - 1) // MOE_RB, block, 0)

    @pl.when(e == pl.num_programs(1) - 1)
    def _():
        pm = pm_scr[...]
        for cb in range(D_MODEL // MOE_CB):
            cols = slice(cb * MOE_CB, (cb + 1) * MOE_CB)
            o_ref[:, cols] = jnp.dot(pm, o_ref[:, cols].astype(BF16), preferred_element_type=F32)


def _moe(h2, comb, offs, w_gate, w_up, w_down, layer):
    n = h2.shape[0]
    grid_spec = pltpu.PrefetchScalarGridSpec(
        num_scalar_prefetch=1,
        grid=(n // MOE_TM, MOE_EXPERTS),
        in_specs=[pl.BlockSpec((MOE_TM, D_MODEL), lambda i, e, offs: (i, 0)),
                  pl.BlockSpec((MOE_TM, LANES), lambda i, e, offs: (i, 0)),
                  pl.BlockSpec((None, None, D_MODEL, MOE_HID), lambda i, e, offs: (layer, e, 0, 0)),
                  pl.BlockSpec((None, None, D_MODEL, MOE_HID), lambda i, e, offs: (layer, e, 0, 0)),
                  pl.BlockSpec((None, None, MOE_HID, D_MODEL), lambda i, e, offs: (layer, e, 0, 0))],
        out_specs=pl.BlockSpec((MOE_TM, D_MODEL), lambda i, e, offs: (i, 0)),
        scratch_shapes=[pltpu.VMEM((MOE_TM, D_MODEL), BF16), pltpu.VMEM((MOE_TM, LANES), F32),
                        pltpu.VMEM((MOE_TM, MOE_TM), BF16), pltpu.VMEM((D_MODEL, MOE_HID), BF16),
                        pltpu.VMEM((D_MODEL, MOE_HID), BF16), pltpu.VMEM((MOE_HID, D_MODEL), BF16)])
    return pl.pallas_call(
        _moe_kernel,
        grid_spec=grid_spec,
        out_shape=jax.ShapeDtypeStruct((n, D_MODEL), F32),
        compiler_params=_cparams(("arbitrary", "arbitrary")),
        name="moe_experts",
    )(offs, h2, comb, w_gate, w_up, w_down)


def _group_offsets(cnt):
    per = MOE_TM // OUT_TM
    c = cnt[:, 0, :MOE_GROUPS].astype(jnp.int32).reshape(-1, per, MOE_GROUPS).sum(axis=1)
    return jnp.concatenate([jnp.zeros((c.shape[0], 1), jnp.int32), jnp.cumsum(c, axis=1)], axis=1)


FIN_TM = 512


def _final_kernel(x_ref, f_ref, mod_ref, lng_ref, lnb_ref, o_ref, *, req_base, rows_per_req):
    i = pl.program_id(0)
    m = req_base + (i * FIN_TM) // rows_per_req
    g2 = mod_ref[5, pl.ds(m, 1), :]
    o_ref[...] = _ln(ALPHA * x_ref[...] + g2 * f_ref[...]) * lng_ref[1:2, :] + lnb_ref[1:2, :]


def _final(x1, ffn, mod_l, ln_g, ln_b, layer, req_base, rows_per_req):
    n = x1.shape[0]
    row = pl.BlockSpec((FIN_TM, D_MODEL), lambda i: (i, 0))
    lnspec = pl.BlockSpec((None, 2, D_MODEL), lambda i: (layer, 0, 0))
    return pl.pallas_call(
        functools.partial(_final_kernel, req_base=req_base, rows_per_req=rows_per_req),
        grid=(n // FIN_TM,),
        in_specs=[row, row, pl.BlockSpec(mod_l.shape, lambda i: (0, 0, 0)), lnspec, lnspec],
        out_specs=row,
        out_shape=jax.ShapeDtypeStruct((n, D_MODEL), F32),
        compiler_params=_cparams(("arbitrary",)),
        name="final_norm",
    )(x1, ffn, mod_l, ln_g, ln_b)


def _block_diag_ones(width, blk):
    i = jnp.arange(width) // blk
    return (i[:, None] == i[None, :]).astype(BF16)


def _layer(x2d, layer, batch, seq, req_base, rows_per_req, mod_l, wts, consts, ctx):
    decode = ctx is not None
    p, h = _inproj(x2d, mod_l, wts['w_in'], layer, req_base, rows_per_req)

    ret = _retention(p, consts['ret_dec'][layer], ctx['ret'] if decode else None, batch, seq,
                     emit_state=not decode)
    cache = (ctx['k'], ctx['v'], consts['rope_cos'], consts['rope_sin']) if decode else None
    att = _attention(p, consts['gq'][layer], consts['gk'][layer], consts['bd_q'], consts['bd_k'],
                     batch, seq, cache)
    hc, hs = consts['hy_filt'][seq][layer]
    o_c = _hyena(p, wts['hy_conv'], wts['hy_bias'], hc, hs, consts['dft'][seq], layer, batch, seq)
    o_d, rw_state = _rwkv(p, consts['rw_mu'][layer], consts['rw_aup'][layer], wts['rw_g_up'][layer],
                          consts['rw_wup'][layer], wts['rw_w0'][layer], consts['rw_a0'][layer],
                          consts['rw_kk'][layer], consts['rw_ka'][layer], consts['rw_rk'][layer],
                          consts['bd_pair'], ctx['rw'] if decode else None, batch, seq)

    merged = _merge(h, (ret[0], att[0], o_c, o_d), wts['w_in'], wts['w_branch'], layer)
    x1, h2, comb, cnt = _outproj(merged, consts['w_out16'], x2d, mod_l, wts['ln_g'], wts['ln_b'],
                            consts['router_hi'][layer], consts['router_lo'][layer], consts['router_b'][layer],
                            layer, req_base, rows_per_req)
    ffn = _moe(h2, comb, _group_offsets(cnt), wts['moe_w_gate'], wts['moe_w_up'], wts['moe_w_down'], layer)
    x2 = _final(x1, ffn, mod_l, wts['ln_g'], wts['ln_b'], layer, req_base, rows_per_req)
    if decode:
        return x2, None
    return x2, (att[1], att[2], ret[1], rw_state)


def kernel(x_prompt, x_sample, c, cache_attn_k, cache_attn_v, state_ret, state_rwkv, c_ctx, mod_w, mod_b, w_in, ret_decay_exp, attn_q_norm, attn_k_norm, hy_conv, hy_w1, hy_b1, hy_w2, hy_b2, hy_freq, hy_w3, hy_log_rate, hy_bias, rw_mu, rw_w0, rw_w_up, rw_a0, rw_a_up, rw_g_up, rw_k_k, rw_k_a, rw_r_k, w_branch, w_out, ln_g, ln_b, moe_rg_w, moe_rg_b, moe_re_w, moe_re_b, moe_w_gate, moe_w_up, moe_w_down):
    batch, seq, _ = x_prompt.shape
    dbatch, dseq, _ = x_sample.shape
    past = cache_attn_k.shape[2]
    kvw = ATT_KV_HEADS * ATT_HD

    cvec = jnp.concatenate([c_ctx[None, :], c, jnp.zeros((8 - 1 - dbatch, D_MODEL), F32)], axis=0)
    mod = _modulation(cvec, mod_w, mod_b)

    wts = dict(w_in=w_in, hy_conv=hy_conv, hy_bias=hy_bias, rw_g_up=rw_g_up, rw_w0=rw_w0, w_branch=w_branch,
               ln_g=ln_g, ln_b=ln_b, moe_w_gate=moe_w_gate, moe_w_up=moe_w_up,
               moe_w_down=moe_w_down)

    zlo = jnp.zeros((DEPTH, RW_N, RW_C), F32)
    router_pad = LANES - MOE_EXPERTS - MOE_GROUPS
    router_w = jnp.pad(jnp.concatenate([moe_re_w, moe_rg_w], axis=2), ((0, 0), (0, 0), (0, router_pad)))
    router_hi = router_w.astype(BF16)
    router_lo = (router_w - router_hi.astype(F32)).astype(BF16)
    rope_cos, rope_sin = _rope_tables(dseq)
    consts = dict(
        ret_dec=jnp.repeat(ret_decay_exp, RET_DV, axis=-1),
        gq=jnp.tile(attn_q_norm, (1, ATT_HEADS))[:, None, :],
        gk=jnp.tile(attn_k_norm, (1, ATT_KV_HEADS))[:, None, :],
        bd_q=_block_diag_ones(ATT_HEADS * ATT_HD, ATT_HD),
        bd_k=_block_diag_ones(kvw, ATT_HD),
        bd_pair=_block_diag_ones(LANES, RW_N),
        rope_cos=rope_cos, rope_sin=rope_sin,
        rw_mu=rw_mu[:, None, :],
        rw_aup=jnp.concatenate([zlo, rw_a_up], axis=1),
        rw_wup=jnp.concatenate([rw_w_up, jnp.zeros_like(rw_w_up)], axis=2),
        rw_a0=rw_a0[:, None, :], rw_kk=rw_k_k[:, None, :], rw_ka=rw_k_a[:, None, :],
        rw_rk=rw_r_k.reshape(DEPTH, 1, RW_C),
        w_out16=w_out.astype(BF16),
        router_hi=router_hi, router_lo=router_lo,
        router_b=jnp.pad(jnp.concatenate([moe_re_b, moe_rg_b], axis=1), ((0, 0), (0, router_pad)))[:, None, :],
        dft={}, hy_filt={},
    )
    w1p = jnp.pad(hy_w1, ((0, 0), (0, LANES - HY_FEAT), (0, 0)))
    for s in sorted({seq, dseq}):
        dft = _dft_matrices(s)
        z = _hyena_feats(s)
        consts['dft'][s] = dft
        consts['hy_filt'][s] = [
            _hyena_filters(z, w1p[l], hy_b1[l][None, :], hy_w2[l], hy_b2[l][None, :], hy_freq[l], hy_w3[l],
                           hy_log_rate[l].reshape(1, 2 * HY_C), dft, s)
            for l in range(DEPTH)]

    y = x_prompt.reshape(batch * seq, D_MODEL)
    ks, vs, rets, rws = [], [], [], []
    for l in range(DEPTH):
        y, (k_l, v_l, ret_l, rw_l) = _layer(y, l, batch, seq, 0, batch * seq, mod[l], wts, consts, None)
        ks.append(k_l)
        vs.append(v_l)
        rets.append(ret_l)
        rws.append(_unpair_states(rw_l))
    y_prompt = y.reshape(batch, seq, D_MODEL)
    new_k = jnp.stack(ks, 1).reshape(batch, DEPTH, seq, ATT_KV_HEADS, ATT_HD)
    new_v = jnp.stack(vs, 1).reshape(batch, DEPTH, seq, ATT_KV_HEADS, ATT_HD)
    new_ret = jnp.stack(rets, 1)
    new_rw = jnp.stack(rws, 1)

    ys = x_sample.reshape(dbatch * dseq, D_MODEL)
    for l in range(DEPTH):
        ctx = dict(k=cache_attn_k[:, l].reshape(dbatch, past, kvw), v=cache_attn_v[:, l].reshape(dbatch, past, kvw),
                   ret=state_ret[:, l], rw=_pair_states(state_rwkv[:, l]))
        ys, _ = _layer(ys, l, dbatch, dseq, 1, dseq, mod[l], wts, consts, ctx)
    y_sample = ys.reshape(dbatch, dseq, D_MODEL)

    return (y_prompt, y_sample, new_k, new_v, new_ret, new_rw)
```

```python
import functools
import math

import jax
import jax.numpy as jnp
from jax import lax
from jax.experimental import pallas as pl
from jax.experimental.pallas import tpu as pltpu

F32 = jnp.float32
BF16 = jnp.bfloat16

D_MODEL = 2048
DEPTH = 4
GRID_W = 64
RET_CHUNK = 128
EPS = 1e-5
RW_LN_EPS = 64e-5
RET_HEADS = 4
RET_DK = 128
RET_DV = 128
ATT_HEADS = 8
ATT_KV_HEADS = 2
ATT_HD = 64
ROPE_THETA = 10000.0
HY_C = 512
HY_BANDS = 8
HY_FEAT = 1 + 2 * HY_BANDS
HY_HID = 64
RW_HEADS = 8
RW_N = 64
RW_C = RW_HEADS * RW_N
RW_CHUNK = 64
N_BRANCH = 4
BRANCH_W = 512
MOE_GROUPS = 4
MOE_PER_GROUP = 8
MOE_EXPERTS = MOE_GROUPS * MOE_PER_GROUP
MOE_HID = 256
N_MOD = 6
ALPHA = (2.0 * DEPTH) ** 0.25

MIX_W = 6144
COL_TILE = 256
LANES = 128
VMEM_LIMIT = 56 * 1024 * 1024

T_RQ, T_RK, T_RV, T_RG = 0, 2, 4, 6
T_AQ, T_AKV = 8, 10
T_HY = 11
T_RW = 17


def _cparams(sem):
    return pltpu.CompilerParams(dimension_semantics=sem, vmem_limit_bytes=VMEM_LIMIT)


def _dot(a, b):
    return jnp.dot(a.astype(BF16), b.astype(BF16), preferred_element_type=F32)


def _dot_nt(a, b):
    return lax.dot_general(a.astype(BF16), b.astype(BF16), (((1,), (1,)), ((), ())), preferred_element_type=F32)


def _dot_tn(a, b):
    return lax.dot_general(a.astype(BF16), b.astype(BF16), (((0,), (0,)), ((), ())), preferred_element_type=F32)


def _split(x):
    hi = x.astype(BF16)
    lo = (x - hi.astype(F32)).astype(BF16)
    return hi, lo


def _dot_x2(x, m):
    hi, lo = _split(x)
    return jnp.dot(hi, m, preferred_element_type=F32) + jnp.dot(lo, m, preferred_element_type=F32)


def _dot_m2x(m, x):
    hi, lo = _split(x)
    return jnp.dot(m, hi, preferred_element_type=F32) + jnp.dot(m, lo, preferred_element_type=F32)


def _dot3(x, y):
    xh, xl = _split(x)
    yh, yl = _split(y)
    return (jnp.dot(xh, yh, preferred_element_type=F32) + jnp.dot(xl, yh, preferred_element_type=F32)
            + jnp.dot(xh, yl, preferred_element_type=F32))


def _ln(x, eps=EPS):
    mu = jnp.mean(x, axis=-1, keepdims=True)
    xc = x - mu
    var = jnp.mean(xc * xc, axis=-1, keepdims=True)
    return xc * lax.rsqrt(var + eps)


def _silu(x):
    return x * jax.nn.sigmoid(x)


MOD_TN = 1024


def _mod_kernel(c_ref, w_ref, b_ref, o_ref):
    s = _silu(c_ref[...])
    o_ref[...] = _dot(s, w_ref[...]) + b_ref[...]


def _modulation(cvec, mod_w, mod_b):
    per = D_MODEL // MOD_TN
    return pl.pallas_call(
        _mod_kernel,
        grid=(DEPTH, N_MOD * per),
        in_specs=[
            pl.BlockSpec((8, D_MODEL), lambda l, n: (0, 0)),
            pl.BlockSpec((None, D_MODEL, MOD_TN), lambda l, n: (l, 0, n)),
            pl.BlockSpec((None, 1, MOD_TN), lambda l, n: (l, 0, n)),
        ],
        out_specs=pl.BlockSpec((None, None, 8, MOD_TN), lambda l, n: (l, n // per, 0, n % per)),
        out_shape=jax.ShapeDtypeStruct((DEPTH, N_MOD, 8, D_MODEL), F32),
        compiler_params=_cparams(("arbitrary", "arbitrary")),
        name="modulation",
    )(cvec, mod_w, mod_b.reshape(DEPTH, 1, N_MOD * D_MODEL))


INP_TM = 1024
INP_TN = 512


def _inproj_kernel(x_ref, mod_ref, w_ref, p_ref, h_ref, *, req_base, rows_per_req):
    i = pl.program_id(0)
    j = pl.program_id(1)

    @pl.when(j == 0)
    def _():
        m = req_base + (i * INP_TM) // rows_per_req
        sh = mod_ref[0, pl.ds(m, 1), :]
        sc = mod_ref[1, pl.ds(m, 1), :]
        h_ref[...] = (_ln(x_ref[...]) * (1.0 + sc) + sh).astype(BF16)

    p_ref[...] = jnp.dot(h_ref[...], w_ref[...].astype(BF16), preferred_element_type=F32)


def _inproj(x2d, mod_l, w_in, layer, req_base, rows_per_req):
    n = x2d.shape[0]
    kern = functools.partial(_inproj_kernel, req_base=req_base, rows_per_req=rows_per_req)
    return pl.pallas_call(
        kern,
        grid=(n // INP_TM, MIX_W // INP_TN),
        in_specs=[
            pl.BlockSpec((INP_TM, D_MODEL), lambda i, j: (i, 0)),
            pl.BlockSpec((N_MOD, 8, D_MODEL), lambda i, j: (0, 0, 0)),
            pl.BlockSpec((None, D_MODEL, INP_TN), lambda i, j: (layer, 0, j)),
        ],
        out_specs=[
            pl.BlockSpec((INP_TM, INP_TN), lambda i, j: (i, j)),
            pl.BlockSpec((INP_TM, D_MODEL), lambda i, j: (i, 0)),
        ],
        out_shape=[jax.ShapeDtypeStruct((n, MIX_W), F32), jax.ShapeDtypeStruct((n, D_MODEL), BF16)],
        compiler_params=_cparams(("arbitrary", "arbitrary")),
        name="in_projection",
    )(x2d, mod_l, w_in)


def _retention_kernel(*refs, seq, has_s0, emit_state):
    q_ref, k_ref, v_ref, g_ref, dec_ref = refs[:5]
    pos = 5
    s0_ref = None
    if has_s0:
        s0_ref = refs[pos]
        pos += 1
    o_ref = refs[pos]
    pos += 1
    st_ref = None
    if emit_state:
        st_ref = refs[pos]
        pos += 1
    s_scr, o_scr = refs[pos], refs[pos + 1]

    C = RET_CHUNK
    nc = seq // C
    lg_all = jnp.log1p(-jnp.exp2(-dec_ref[...]))
    ii = lax.broadcasted_iota(jnp.int32, (C, C), 0)
    jj = lax.broadcasted_iota(jnp.int32, (C, C), 1)
    rel = (ii - jj).astype(F32)
    icol = lax.broadcasted_iota(jnp.int32, (C, 1), 0).astype(F32)

    for d in range(2):
        for h in range(RET_HEADS):
            if has_s0:
                s_scr[d, h] = s0_ref[d, h]
            else:
                s_scr[d, h] = jnp.zeros((RET_DK, RET_DV), F32)

    consts = {}
    for d in range(2):
        for h in range(RET_HEADS):
            lg = lg_all[d:d + 1, h * RET_DV:(h + 1) * RET_DV]
            lg1 = lg[:, :1]
            if d == 0:
                dmask = jnp.where(rel >= 0, jnp.exp(lg * rel), 0.0)
                q_dec = jnp.exp(lg1 * (icol + 1.0))
                k_dec = jnp.exp(lg1 * (C - 1.0 - icol))
            else:
                dmask = jnp.where(rel <= 0, jnp.exp(lg * (-rel)), 0.0)
                q_dec = jnp.exp(lg1 * (C - icol))
                k_dec = jnp.exp(lg1 * icol)
            consts[d, h] = (dmask, q_dec, k_dec, jnp.exp(lg * float(C)))

    o_scr[...] = jnp.zeros_like(o_scr)

    def body(ci, carry):
        units = []
        for d in range(2):
            c = ci if d == 0 else nc - 1 - ci
            rows = pl.ds(pl.multiple_of(c * C, C), C)
            for h in range(RET_HEADS):
                cs = slice(h * RET_DK, (h + 1) * RET_DK)
                kc = k_ref[rows, cs] * (RET_DK ** -0.5)
                units.append(dict(d=d, h=h, rows=rows, cs=cs, q16=q_ref[rows, cs].astype(BF16), kc=kc,
                                  v16=v_ref[rows, cs].astype(BF16), s=s_scr[d, h]))
        scores = [_dot_nt(u['q16'], u['kc']) for u in units]
        cross = [_dot(u['q16'], u['s']) for u in units]
        kv = [_dot_tn(u['kc'] * consts[u['d'], u['h']][2], u['v16']) for u in units]
        inner = [_dot(sc * consts[u['d'], u['h']][0], u['v16']) for sc, u in zip(scores, units)]
        for u, cr, upd, inn in zip(units, cross, kv, inner):
            _, q_dec, _, c_dec = consts[u['d'], u['h']]
            s_scr[u['d'], u['h']] = u['s'] * c_dec + upd
            o_scr[u['rows'], u['cs']] = o_scr[u['rows'], u['cs']] + (inn + cr * q_dec)
        return carry

    lax.fori_loop(0, nc, body, 0)

    for h in range(RET_HEADS):
        cs = slice(h * RET_DV, (h + 1) * RET_DV)
        o_ref[:, cs] = (_ln(o_scr[:, cs]) * _silu(g_ref[:, cs])).astype(BF16)
    if emit_state:
        st_ref[...] = s_scr[...]


def _retention(p, dec_rep, s0, batch, seq, emit_state):
    n = batch * seq
    w = RET_HEADS * RET_DK
    has_s0 = s0 is not None
    kern = functools.partial(_retention_kernel, seq=seq, has_s0=has_s0, emit_state=emit_state)
    in_specs = [
        pl.BlockSpec((seq, w), lambda b: (b, 0)),
        pl.BlockSpec((seq, w), lambda b: (b, 1)),
        pl.BlockSpec((seq, w), lambda b: (b, 2)),
        pl.BlockSpec((seq, w), lambda b: (b, 3)),
        pl.BlockSpec((2, w), lambda b: (0, 0)),
    ]
    args = [p, p, p, p, dec_rep]
    if has_s0:
        in_specs.append(pl.BlockSpec((None, 2, RET_HEADS, RET_DK, RET_DV), lambda b: (b, 0, 0, 0, 0)))
        args.append(s0)
    out_specs = [pl.BlockSpec((seq, w), lambda b: (b, 0))]
    out_shape = [jax.ShapeDtypeStruct((n, w), BF16)]
    if emit_state:
        out_specs.append(pl.BlockSpec((None, 2, RET_HEADS, RET_DK, RET_DV), lambda b: (b, 0, 0, 0, 0)))
        out_shape.append(jax.ShapeDtypeStruct((batch, 2, RET_HEADS, RET_DK, RET_DV), F32))
    return pl.pallas_call(
        kern,
        grid=(batch,),
        in_specs=in_specs,
        out_specs=out_specs,
        out_shape=out_shape,
        scratch_shapes=[pltpu.VMEM((2, RET_HEADS, RET_DK, RET_DV), F32), pltpu.VMEM((seq, w), F32)],
        compiler_params=_cparams(("arbitrary",)),
        name="retention",
    )(*args)


ATT_QB = 256


def _head_rms(x, ones_bd, gain):
    ss = _dot_x2(x * x, ones_bd)
    return x * lax.rsqrt(ss * (1.0 / ATT_HD) + EPS) * gain


def _rope(x, cos, sin_signed):
    w = x.shape[-1]
    lane = lax.broadcasted_iota(jnp.int32, x.shape, 1)
    first = (lane & 16) == 0
    swapped = jnp.where(first, pltpu.roll(x, w - 16, axis=1), pltpu.roll(x, 16, axis=1))
    return x * cos + swapped * sin_signed


def _attention_kernel(*refs, seq, decode, past):
    q_ref, kv_ref, gq_ref, gk_ref, bdq_ref, bdk_ref = refs[:6]
    pos = 6
    if decode:
        ck_ref, cv_ref, cos_ref, sin_ref = refs[pos:pos + 4]
        pos += 4
    o_ref = refs[pos]
    pos += 1
    if not decode:
        ko_ref, vo_ref = refs[pos:pos + 2]
        pos += 2
    q_scr, k_scr, v_scr = refs[pos:pos + 3]

    kvw = ATT_KV_HEADS * ATT_HD
    q = _head_rms(q_ref[...], bdq_ref[...], gq_ref[...])
    kv = kv_ref[...]
    k = _head_rms(kv[:, :kvw], bdk_ref[...], gk_ref[...])
    v = kv[:, kvw:]
    if decode:
        cos = cos_ref[...]
        sin = sin_ref[...]
        q = _rope(q, cos, sin)
        k = _rope(k, cos[:, :kvw], sin[:, :kvw])
        k_scr[0:past, :] = ck_ref[...].astype(BF16)
        v_scr[0:past, :] = cv_ref[...].astype(BF16)
        k_scr[past:past + seq, :] = k.astype(BF16)
        v_scr[past:past + seq, :] = v.astype(BF16)
    else:
        ko_ref[...] = k
        vo_ref[...] = v
        k_scr[...] = k.astype(BF16)
        v_scr[...] = v.astype(BF16)
    q_scr[...] = (q * (ATT_HD ** -0.5)).astype(BF16)

    grp = ATT_HEADS // ATT_KV_HEADS

    def body(qi, carry):
        r0 = pl.multiple_of(qi * ATT_QB, ATT_QB)
        scores = []
        for g in range(ATT_KV_HEADS):
            qg = jnp.concatenate([q_scr[pl.ds(r0, ATT_QB), h * ATT_HD:(h + 1) * ATT_HD]
                                  for h in range(g * grp, (g + 1) * grp)], axis=0)
            kh = k_scr[:, g * ATT_HD:(g + 1) * ATT_HD]
            scores.append(lax.dot_general(qg, kh, (((1,), (1,)), ((), ())), preferred_element_type=F32))
        outs = []
        for g, s in enumerate(scores):
            s = s - jnp.max(s, axis=-1, keepdims=True)
            e = jnp.exp(s)
            prob = e / jnp.sum(e, axis=-1, keepdims=True)
            og = jnp.dot(prob.astype(BF16), v_scr[:, g * ATT_HD:(g + 1) * ATT_HD], preferred_element_type=F32)
            outs += [og[j * ATT_QB:(j + 1) * ATT_QB, :] for j in range(grp)]
        o_ref[pl.ds(r0, ATT_QB), :] = jnp.concatenate(outs, axis=1).astype(BF16)
        return carry

    lax.fori_loop(0, seq // ATT_QB, body, 0)


def _attention(p, gq, gk, bdq, bdk, batch, seq, cache=None):
    n = batch * seq
    qw = ATT_HEADS * ATT_HD
    kvw = ATT_KV_HEADS * ATT_HD
    decode = cache is not None
    past = cache[0].shape[1] if decode else 0
    kern = functools.partial(_attention_kernel, seq=seq, decode=decode, past=past)
    in_specs = [
        pl.BlockSpec((seq, qw), lambda b: (b, T_AQ * COL_TILE // qw)),
        pl.BlockSpec((seq, 2 * kvw), lambda b: (b, T_AKV * COL_TILE // (2 * kvw))),
        pl.BlockSpec((1, qw), lambda b: (0, 0)),
        pl.BlockSpec((1, kvw), lambda b: (0, 0)),
        pl.BlockSpec((qw, qw), lambda b: (0, 0)),
        pl.BlockSpec((kvw, kvw), lambda b: (0, 0)),
    ]
    args = [p, p, gq, gk, bdq, bdk]
    out_specs = [pl.BlockSpec((seq, qw), lambda b: (b, 0))]
    out_shape = [jax.ShapeDtypeStruct((n, qw), BF16)]
    if decode:
        ck, cv, cos, sin = cache
        in_specs += [
            pl.BlockSpec((None, past, kvw), lambda b: (b, 0, 0)),
            pl.BlockSpec((None, past, kvw), lambda b: (b, 0, 0)),
            pl.BlockSpec((seq, qw), lambda b: (0, 0)),
            pl.BlockSpec((seq, qw), lambda b: (0, 0)),
        ]
        args += [ck, cv, cos, sin]
    else:
        out_specs += [pl.BlockSpec((None, seq, kvw), lambda b: (b, 0, 0))] * 2
        out_shape += [jax.ShapeDtypeStruct((batch, seq, kvw), F32)] * 2
    return pl.pallas_call(
        kern,
        grid=(batch,),
        in_specs=in_specs,
        out_specs=out_specs,
        out_shape=out_shape,
        scratch_shapes=[pltpu.VMEM((seq, qw), BF16), pltpu.VMEM((past + seq, kvw), BF16),
                        pltpu.VMEM((past + seq, kvw), BF16)],
        compiler_params=_cparams(("arbitrary",)),
        name="attention",
    )(*args)


def _rope_tables(seq):
    t = jnp.arange(seq, dtype=jnp.int32)
    row = (t // GRID_W).astype(F32)
    col = (t % GRID_W).astype(F32)
    nf = ATT_HD // 4
    inv = ROPE_THETA ** (-jnp.arange(nf, dtype=F32) / nf)
    a_row = row[:, None] * inv[None, :]
    a_col = col[:, None] * inv[None, :]
    ang = jnp.concatenate([a_row, a_row, a_col, a_col], axis=1)
    sign = jnp.concatenate([-jnp.ones((nf,), F32), jnp.ones((nf,), F32)] * 2)
    cos = jnp.tile(jnp.cos(ang), (1, ATT_HEADS))
    sin = jnp.tile(jnp.sin(ang) * sign[None, :], (1, ATT_HEADS))
    return cos, sin


def _dft_matrices(seq):
    two_l = 2 * seq
    f = jnp.arange(seq, dtype=jnp.int32)
    t = jnp.arange(seq, dtype=jnp.int32)
    m_fwd = (f[:, None] * t[None, :]) % two_l
    ang_fwd = m_fwd.astype(F32) * (math.pi / seq)
    alt_t = jnp.where(t % 2 == 0, 1.0, -1.0).astype(F32)
    fc = jnp.cos(ang_fwd)
    fs = jnp.where(f[:, None] == 0, alt_t[None, :], jnp.sin(ang_fwd))
    n_out = t + seq // 2
    m_inv = (n_out[:, None] * f[None, :]) % two_l
    ang_inv = m_inv.astype(F32) * (math.pi / seq)
    wgt = jnp.where(f == 0, 1.0, 2.0).astype(F32) / two_l
    alt_n = jnp.where(n_out % 2 == 0, 1.0, -1.0).astype(F32)
    gc = jnp.cos(ang_inv) * wgt[None, :]
    gs = jnp.where(f[None, :] == 0, alt_n[:, None] / two_l, jnp.sin(ang_inv) * wgt[None, :])
    return tuple(m.astype(BF16) for m in (fc, fs, gc, gs))


def _hyena_feats(seq):
    t = jnp.arange(seq, dtype=F32) / seq
    bands = jnp.arange(1, HY_BANDS + 1, dtype=F32)
    ang = 2.0 * jnp.pi * t[:, None] * bands[None, :]
    z = jnp.concatenate([t[:, None], jnp.sin(ang), jnp.cos(ang)], -1)
    return jnp.pad(z, ((0, 0), (0, LANES - HY_FEAT)))


def _hyena_filter_kernel(z_ref, w1_ref, b1_ref, w2_ref, b2_ref, fr_ref, w3_ref, lr_ref,
                         fc_ref, fs_ref, hc_ref, hs_ref, *, seq):
    fr = fr_ref[...]
    h = jnp.sin(fr[0:1, :] * (_dot3(z_ref[...], w1_ref[...]) + b1_ref[...]))
    h = jnp.sin(fr[1:2, :] * (_dot3(h, w2_ref[...]) + b2_ref[...]))
    h = _dot3(h, w3_ref[...])
    t = lax.broadcasted_iota(jnp.int32, (seq, 1), 0).astype(F32)
    dist = jnp.abs(t - float(seq // 2)) / (0.5 * seq)
    h = h * jnp.exp(-jnp.exp(lr_ref[...]) * dist)
    h16 = h.astype(BF16)
    hc_ref[...] = jnp.dot(fc_ref[...], h16, preferred_element_type=F32)
    hs_ref[...] = jnp.dot(fs_ref[...], h16, preferred_element_type=F32)


def _hyena_filters(z, w1p, b1, w2, b2, fr, w3, lr, dft, seq):
    fc, fs = dft[:2]
    full = lambda a: pl.BlockSpec(a.shape, lambda: (0,) * a.ndim)
    args = [z, w1p, b1, w2, b2, fr, w3, lr, fc, fs]
    return pl.pallas_call(
        functools.partial(_hyena_filter_kernel, seq=seq),
        in_specs=[full(a) for a in args],
        out_specs=[pl.BlockSpec((seq, 2 * HY_C), lambda: (0, 0))] * 2,
        out_shape=[jax.ShapeDtypeStruct((seq, 2 * HY_C), F32)] * 2,
        compiler_params=pltpu.CompilerParams(vmem_limit_bytes=VMEM_LIMIT),
        name="hyena_filters",
    )(*args)


def _shift_rows(u, seq):
    row = lax.broadcasted_iota(jnp.int32, u.shape, 0)
    prev = jnp.where(row == 0, 0.0, pltpu.roll(u, 1, axis=0))
    nxt = jnp.where(row == seq - 1, 0.0, pltpu.roll(u, seq - 1, axis=0))
    return prev, nxt


def _hyena_kernel(v_ref, x1_ref, x2_ref, cv_ref, c1_ref, c2_ref, bias_ref,
                  hc0_ref, hs0_ref, hc1_ref, hs1_ref,
                  fc_ref, fs_ref, gc_ref, gs_ref, o_ref, *, seq):
    def dwconv(u_ref, w_ref):
        u = u_ref[...]
        w = w_ref[...]
        prev, nxt = _shift_rows(u, seq)
        return prev * w[0:1, :] + u * w[1:2, :] + nxt * w[2:3, :]

    row0 = lax.broadcasted_iota(jnp.int32, (seq, COL_TILE), 0) == 0

    def long_conv(u, hc, hs):
        u16 = u.astype(BF16)
        uc = jnp.dot(fc_ref[...], u16, preferred_element_type=F32)
        us = jnp.dot(fs_ref[...], u16, preferred_element_type=F32)
        ss = us * hs
        yc = uc * hc - jnp.where(row0, 0.0, ss)
        ys = jnp.where(row0, ss, uc * hs + us * hc)
        return (jnp.dot(gc_ref[...], yc.astype(BF16), preferred_element_type=F32)
                + jnp.dot(gs_ref[...], ys.astype(BF16), preferred_element_type=F32))

    bias = bias_ref[...]
    v = dwconv(v_ref, cv_ref)
    z = long_conv(v, hc0_ref[...], hs0_ref[...]) + bias[0:1, :] * v
    z = dwconv(x1_ref, c1_ref) * z
    z = long_conv(z, hc1_ref[...], hs1_ref[...]) + bias[1:2, :] * z
    o_ref[...] = (dwconv(x2_ref, c2_ref) * z).astype(BF16)


def _hyena(p, hy_conv, hy_bias, hc, hs, dft, layer, batch, seq):
    n = batch * seq
    nb = HY_C // COL_TILE
    tile = lambda off: pl.BlockSpec((seq, COL_TILE), lambda b, c: (b, T_HY + off * nb + c))
    cw = lambda off: pl.BlockSpec((None, 3, COL_TILE), lambda b, c: (layer, 0, off * nb + c))
    filt = lambda o: pl.BlockSpec((seq, COL_TILE), lambda b, c: (0, o * nb + c))
    mat = pl.BlockSpec((seq, seq), lambda b, c: (0, 0))
    return pl.pallas_call(
        functools.partial(_hyena_kernel, seq=seq),
        grid=(batch, nb),
        in_specs=[tile(0), tile(1), tile(2), cw(0), cw(1), cw(2),
                  pl.BlockSpec((None, 2, COL_TILE), lambda b, c: (layer, 0, c)),
                  filt(0), filt(0), filt(1), filt(1)] + [mat] * 4,
        out_specs=pl.BlockSpec((seq, COL_TILE), lambda b, c: (b, c)),
        out_shape=jax.ShapeDtypeStruct((n, HY_C), BF16),
        compiler_params=_cparams(("arbitrary", "arbitrary")),
        name="hyena",
    )(p, p, p, hy_conv, hy_conv, hy_conv, hy_bias, hc, hs, hc, hs, *dft)


def _stack2(x):
    lane = lax.broadcasted_iota(jnp.int32, x.shape, 1)
    first = lane < RW_N
    return jnp.concatenate([jnp.where(first, x, 0.0), jnp.where(first, 0.0, x)], axis=0)


def _rwkv_kernel(*refs, seq, has_s0):
    (pr_ref, pk_ref, pv_ref, pl_ref, mur_ref, muk_ref, muv_ref, mul_ref, aup_ref, gup_ref, wup_ref, w0_ref,
     a0_ref, kk_ref, ka_ref, rk_ref, bd_ref) = refs[:17]
    pos = 17
    s0_ref = None
    if has_s0:
        s0_ref = refs[pos]
        pos += 1
    o_ref, st_ref = refs[pos], refs[pos + 1]
    r_s, k_s, v_s, a_s, b_s, gate_s, bonus_s, y_scr, e_s, pre16, pre32 = refs[pos + 2:pos + 13]

    def shifted(ref, mu_ref):
        p = ref[...]
        prev, nxt = _shift_rows(p, seq)
        return p + mu_ref[...] * (0.5 * (prev + nxt) - p)

    bd = bd_ref[...]
    r = shifted(pr_ref, mur_ref)
    k = shifted(pk_ref, muk_ref)
    v = shifted(pv_ref, muv_ref)
    low = shifted(pl_ref, mul_ref)
    wa = low[:, :LANES]
    iclr = jax.nn.sigmoid(a0_ref[...] + _dot(wa, aup_ref[...]))
    gate_s[...] = _dot(jax.nn.sigmoid(low[:, LANES:]), gup_ref[...])
    kk = k * kk_ref[...]
    kk = kk * lax.rsqrt(_dot_x2(kk * kk, bd) + 1e-12)
    k2 = k * (1.0 + (iclr - 1.0) * ka_ref[...])
    bonus_s[...] = _dot_x2(r * k2 * rk_ref[...], bd) * v
    r_s[...] = r
    k_s[...] = k2
    v_s[...] = v
    a_s[...] = -kk
    b_s[...] = kk * iclr
    tw = jnp.tanh(wa)
    for d in range(2):
        x = -(w0_ref[d:d + 1, :] + _dot(tw, wup_ref[d]))
        softplus = jnp.maximum(x, 0.0) + jnp.log1p(jnp.exp(-jnp.abs(x)))
        e_s[d] = jnp.exp(-softplus - 0.5)

    C = RW_CHUNK
    S = 2 * C
    nc = seq // C
    ri = lax.broadcasted_iota(jnp.int32, (S, S), 0)
    ci = lax.broadcasted_iota(jnp.int32, (S, S), 1)
    eye = ri == ci
    eye_f = jnp.where(eye, 1.0, 0.0)
    tr = ri & (C - 1)
    tc = ci & (C - 1)
    ti = lax.broadcasted_iota(jnp.int32, (C, C), 0)
    tj = lax.broadcasted_iota(jnp.int32, (C, C), 1)
    dir_consts = (
        (tc < tr, tc <= tr, (tj <= ti).astype(BF16), C - 1),
        (tc > tr, tc >= tr, (tj >= ti).astype(BF16), 0),
    )
    level_masks = []
    m = 1
    while m < C:
        lg = m.bit_length() - 1
        same = jnp.right_shift(tr, lg + 1) == jnp.right_shift(tc, lg + 1)
        halves = (jnp.right_shift(tr, lg) & 1) != (jnp.right_shift(tc, lg) & 1)
        level_masks.append(same & halves)
        m *= 2

    def bf(x):
        return x.astype(BF16)

    def mm(x, y):
        return jnp.dot(x, y, preferred_element_type=F32)

    def mm_nt(x, y):
        return lax.dot_general(x, y, (((1,), (1,)), ((), ())), preferred_element_type=F32)

    def mm_tn(x, y):
        return lax.dot_general(x, y, (((0,), (0,)), ((), ())), preferred_element_type=F32)

    def phase1(units):
        fr = []
        row_sl = [pl.ds(pl.multiple_of(c * C, C), C) for _, c in units]
        ecs = [e_s[d, rows, :] for (d, _), rows in zip(units, row_sl)]
        cums = [_dot_m2x(dir_consts[d][2], ec) for (d, _), ec in zip(units, ecs)]
        for (d, c), rows, ec, cum in zip(units, row_sl, ecs, cums):
            strict, incl, tri, last = dir_consts[d]
            g_c = cum[last:last + 1, :]
            inv = jnp.exp(cum)
            to_end = jnp.exp(cum - g_c)
            a16 = bf(_stack2(a_s[rows, :] * jnp.exp(ec - cum)))
            r_t = _stack2(r_s[rows, :] * jnp.exp(-cum))
            bc = b_s[rows, :]
            kc = k_s[rows, :]
            gm = mm_nt(jnp.concatenate([a16, bf(r_t)], axis=0),
                       jnp.concatenate([bf(_stack2(bc * inv)), bf(_stack2(kc * inv))], axis=0))
            fr.append(dict(
                d=d, c=c, g_c=g_c, a16=a16, r_t=r_t,
                bh16=bf(_stack2(bc * to_end)), kh16=bf(_stack2(kc * to_end)), v16=bf(_stack2(v_s[rows, :])),
                n_m=jnp.where(strict, gm[:S, :S], 0.0),
                ak16=bf(jnp.where(strict, gm[:S, S:], 0.0)),
                rb16=bf(jnp.where(incl, gm[S:, :S], 0.0)),
                rk16=bf(jnp.where(incl, gm[S:, S:], 0.0))))
        nu = len(fr)
        ts = [None] * nu
        for li, mask in enumerate(level_masks):
            if li == 0:
                ts = [eye_f + jnp.where(mask, f['n_m'], 0.0) for f in fr]
                continue
            t16 = [bf(t) for t in ts]
            inner = [mm(bf(jnp.where(mask, fr[u]['n_m'], 0.0)), t16[u]) for u in range(nu)]
            ts = [ts[u] + mm(t16[u], bf(inner[u])) for u in range(nu)]
        akv = [mm(f['ak16'], f['v16']) for f in fr]
        wu = [mm(bf(ts[u]), jnp.concatenate([fr[u]['a16'], bf(akv[u])], axis=1)) for u in range(nu)]
        w16 = [bf(x[:, :S]) for x in wu]
        uv16 = [jnp.concatenate([bf(wu[u][:, S:]), fr[u]['v16']], axis=0) for u in range(nu)]
        for u, f in enumerate(fr):
            d, c = f['d'], f['c']
            pre16[d, c, 0] = bf(f['r_t'] + mm(f['rb16'], w16[u]))
            pre32[d, c, 0] = mm(jnp.concatenate([f['rb16'], f['rk16']], axis=1), uv16[u])
            pre16[d, c, 1] = bf(jnp.where(eye, jnp.exp(-f['g_c']), 0.0) + mm_tn(w16[u], f['bh16']))
            pre32[d, c, 1] = mm_tn(uv16[u], jnp.concatenate([f['bh16'], f['kh16']], axis=0))

    group = min(nc, 4)
    if nc == group:
        phase1([(d, c) for d in range(2) for c in range(nc)])
    else:
        def phase1_body(i, carry):
            phase1([(d, i * group + j) for d in range(2) for j in range(group)])
            return carry

        lax.fori_loop(0, nc // group, phase1_body, 0)

    for d in range(2):
        st_ref[d] = s0_ref[d] if has_s0 else jnp.zeros((S, S), F32)
    y_scr[...] = jnp.zeros_like(y_scr)

    def phase2(i):
        cs = (i, nc - 1 - i)
        s16 = [bf(st_ref[d]) for d in range(2)]
        for d in range(2):
            st_ref[d] = mm(s16[d], pre16[d, cs[d], 1]) + pre32[d, cs[d], 1]
        for d in range(2):
            rows = pl.ds(pl.multiple_of(cs[d] * C, C), C)
            y_st = mm_nt(pre16[d, cs[d], 0], s16[d]) + pre32[d, cs[d], 0]
            y_scr[rows, :] = y_scr[rows, :] + (y_st[:C, :] + y_st[C:, :])

    if nc == group:
        for i in range(nc):
            phase2(i)
    else:
        def phase2_body(i, carry):
            phase2(i)
            return carry

        lax.fori_loop(0, nc, phase2_body, 0)

    y = y_scr[...]
    ones_bd = bd_ref[...]
    mean = _dot_x2(y, ones_bd) * (1.0 / RW_N)
    yc = y - mean
    var = _dot_x2(yc * yc, ones_bd) * (1.0 / RW_N)
    yn = yc * lax.rsqrt(var + RW_LN_EPS)
    o_ref[...] = ((yn + bonus_s[...]) * gate_s[...]).astype(BF16)


def _rwkv(p, mu, aup, gup, wup, w0, a0, kk, ka, rk, bd2, s0, batch, seq):
    n = batch * seq
    npair = RW_HEADS // 2
    has_s0 = s0 is not None
    S = 2 * RW_CHUNK
    base = T_RW * COL_TILE // LANES
    per = RW_C // LANES
    pcol = lambda sec: pl.BlockSpec((seq, LANES), lambda b, h: (b, base + sec * per + h))
    mucol = lambda sec: pl.BlockSpec((1, LANES), lambda b, h: (0, sec * per + h))
    vec = pl.BlockSpec((1, LANES), lambda b, h: (0, h))
    low_idx = 3 * RW_C // COL_TILE
    in_specs = [
        pcol(0), pcol(1), pcol(2),
        pl.BlockSpec((seq, COL_TILE), lambda b, h: (b, T_RW + low_idx)),
        mucol(0), mucol(1), mucol(2),
        pl.BlockSpec((1, COL_TILE), lambda b, h: (0, low_idx)),
        pl.BlockSpec((LANES, LANES), lambda b, h: (0, h)),
        pl.BlockSpec((LANES, LANES), lambda b, h: (0, h)),
        pl.BlockSpec((2, LANES, LANES), lambda b, h: (0, 0, h)),
        pl.BlockSpec((2, LANES), lambda b, h: (0, h)),
        vec, vec, vec, vec,
        pl.BlockSpec((LANES, LANES), lambda b, h: (0, 0)),
    ]
    args = [p, p, p, p, mu, mu, mu, mu, aup, gup, wup, w0, a0, kk, ka, rk, bd2]
    if has_s0:
        in_specs.append(pl.BlockSpec((None, 2, None, S, S), lambda b, h: (b, 0, h, 0, 0)))
        args.append(s0)
    seq_buf = pltpu.VMEM((seq, LANES), F32)
    return pl.pallas_call(
        functools.partial(_rwkv_kernel, seq=seq, has_s0=has_s0),
        grid=(batch, npair),
        in_specs=in_specs,
        out_specs=[pl.BlockSpec((seq, LANES), lambda b, h: (b, h)),
                   pl.BlockSpec((None, 2, None, S, S), lambda b, h: (b, 0, h, 0, 0))],
        out_shape=[jax.ShapeDtypeStruct((n, RW_C), BF16), jax.ShapeDtypeStruct((batch, 2, npair, S, S), F32)],
        scratch_shapes=[seq_buf] * 8 + [pltpu.VMEM((2, seq, LANES), F32),
                                        pltpu.VMEM((2, seq // RW_CHUNK, 2, S, S), BF16),
                                        pltpu.VMEM((2, seq // RW_CHUNK, 2, S, S), F32)],
        compiler_params=_cparams(("arbitrary", "arbitrary")),
        name="rwkv7",
    )(*args)


def _pair_states(s):
    b = s.shape[0]
    s = s.reshape(b, 2, RW_HEADS // 2, 2, RW_N, RW_N)
    z = jnp.zeros_like(s[:, :, :, 0])
    top = jnp.concatenate([s[:, :, :, 0], z], axis=-1)
    bot = jnp.concatenate([z, s[:, :, :, 1]], axis=-1)
    return jnp.concatenate([top, bot], axis=-2)


def _unpair_states(s):
    b = s.shape[0]
    h0 = s[:, :, :, :RW_N, :RW_N]
    h1 = s[:, :, :, RW_N:, RW_N:]
    return jnp.stack([h0, h1], axis=3).reshape(b, 2, RW_HEADS, RW_N, RW_N)


MRG_TM = 1024
MRG_TN = 256


def _merge_kernel(h_ref, ba_ref, bb_ref, bc_ref, bd_ref, g0_ref, g1_ref, g2_ref, g3_ref, wb_ref, o_ref):
    h = h_ref[...]
    acc = None
    for i, (br, gw) in enumerate(((ba_ref, g0_ref), (bb_ref, g1_ref), (bc_ref, g2_ref), (bd_ref, g3_ref))):
        gate = jax.nn.sigmoid(jnp.dot(h, gw[...].astype(BF16), preferred_element_type=F32))
        proj = jnp.dot(br[...], wb_ref[i].astype(BF16), preferred_element_type=F32)
        acc = gate * proj if acc is None else acc + gate * proj
    o_ref[...] = acc.astype(BF16)


def _merge(h, branches, w_in, w_branch, layer):
    n = h.shape[0]
    gate_spec = lambda b: pl.BlockSpec(
        (None, D_MODEL, MRG_TN), lambda i, j: (layer, 0, (MIX_W + b * D_MODEL) // MRG_TN + j))
    return pl.pallas_call(
        _merge_kernel,
        grid=(n // MRG_TM, D_MODEL // MRG_TN),
        in_specs=[pl.BlockSpec((MRG_TM, D_MODEL), lambda i, j: (i, 0))]
        + [pl.BlockSpec((MRG_TM, BRANCH_W), lambda i, j: (i, 0))] * N_BRANCH
        + [gate_spec(b) for b in range(N_BRANCH)]
        + [pl.BlockSpec((None, N_BRANCH, BRANCH_W, MRG_TN), lambda i, j: (layer, 0, 0, j))],
        out_specs=pl.BlockSpec((MRG_TM, MRG_TN), lambda i, j: (i, j)),
        out_shape=jax.ShapeDtypeStruct((n, D_MODEL), BF16),
        compiler_params=_cparams(("arbitrary", "arbitrary")),
        name="gated_merge",
    )(h, *branches, w_in, w_in, w_in, w_in, w_branch)


OUT_TM = 256


def _route(logits):
    lane_i = lax.broadcasted_iota(jnp.int32, logits.shape, 1)
    lane = lane_i.astype(F32)
    grp_of_lane = jnp.right_shift(lane_i, MOE_PER_GROUP.bit_length() - 1).astype(F32)
    neg = -jnp.inf
    is_grp = (lane_i >= MOE_EXPERTS) & (lane_i < MOE_EXPERTS + MOE_GROUPS)
    gl = jnp.where(is_grp, logits, neg)
    gmax = jnp.max(gl, axis=-1, keepdims=True)
    gsel = jnp.min(jnp.where(gl == gmax, lane, float(LANES)), axis=-1, keepdims=True) - float(MOE_EXPERTS)
    gval = 1.0 / jnp.sum(jnp.exp(gl - gmax), axis=-1, keepdims=True)
    in_grp = (grp_of_lane == gsel) & (lane_i < MOE_EXPERTS)
    el = jnp.where(in_grp, logits, neg)
    ee = jnp.exp(el - jnp.max(el, axis=-1, keepdims=True))
    prob = jnp.where(in_grp, ee / jnp.sum(ee, axis=-1, keepdims=True), -1.0)
    v1 = jnp.max(prob, axis=-1, keepdims=True)
    i1 = jnp.min(jnp.where(prob == v1, lane, float(LANES)), axis=-1, keepdims=True)
    prob2 = jnp.where(lane == i1, -1.0, prob)
    v2 = jnp.max(prob2, axis=-1, keepdims=True)
    i2 = jnp.min(jnp.where(prob2 == v2, lane, float(LANES)), axis=-1, keepdims=True)
    tot = v1 + v2
    comb = jnp.where(lane == i1, gval * (v1 / tot), 0.0) + jnp.where(lane == i2, gval * (v2 / tot), 0.0)
    comb = comb + jnp.where(lane_i == LANES - 1, gsel, 0.0)
    return comb, jnp.where(lane == gsel, 1.0, 0.0)


def _outproj_kernel(m_ref, w_ref, x_ref, mod_ref, lng_ref, lnb_ref, rwh_ref, rwl_ref, rb_ref,
                    x1_ref, h2_ref, comb_ref, cnt_ref, *, req_base, rows_per_req):
    i = pl.program_id(0)
    mix = jnp.dot(m_ref[...], w_ref[...], preferred_element_type=F32)
    m = req_base + (i * OUT_TM) // rows_per_req
    g1 = mod_ref[2, pl.ds(m, 1), :]
    sh2 = mod_ref[3, pl.ds(m, 1), :]
    sc2 = mod_ref[4, pl.ds(m, 1), :]
    x1 = _ln(ALPHA * x_ref[...] + g1 * mix) * lng_ref[0:1, :] + lnb_ref[0:1, :]
    x1_ref[...] = x1
    h2 = _ln(x1) * (1.0 + sc2) + sh2
    h2_ref[...] = h2.astype(BF16)
    hh, hl = _split(h2)
    rwh = rwh_ref[...]
    logits = (jnp.dot(hh, rwh, preferred_element_type=F32) + jnp.dot(hl, rwh, preferred_element_type=F32)
              + jnp.dot(hh, rwl_ref[...], preferred_element_type=F32)) + rb_ref[...]
    comb, grp_onehot = _route(logits)
    comb_ref[...] = comb
    cnt_ref[...] = jnp.broadcast_to(jnp.sum(grp_onehot, axis=0, keepdims=True), cnt_ref.shape)


def _outproj(merged, w_out16, x2d, mod_l, ln_g, ln_b, rwh, rwl, rb, layer, req_base, rows_per_req):
    n = x2d.shape[0]
    kern = functools.partial(_outproj_kernel, req_base=req_base, rows_per_req=rows_per_req)
    row = lambda w: pl.BlockSpec((OUT_TM, w), lambda i: (i, 0))
    full = lambda a: pl.BlockSpec(a.shape, lambda i: (0,) * a.ndim)
    lnspec = pl.BlockSpec((None, 2, D_MODEL), lambda i: (layer, 0, 0))
    return pl.pallas_call(
        kern,
        grid=(n // OUT_TM,),
        in_specs=[row(D_MODEL),
                  pl.BlockSpec((None, D_MODEL, D_MODEL), lambda i: (layer, 0, 0)),
                  row(D_MODEL), full(mod_l), lnspec, lnspec, full(rwh), full(rwl), full(rb)],
        out_specs=[row(D_MODEL), row(D_MODEL), row(LANES), pl.BlockSpec((None, 8, LANES), lambda i: (i, 0, 0))],
        out_shape=[jax.ShapeDtypeStruct((n, D_MODEL), F32), jax.ShapeDtypeStruct((n, D_MODEL), BF16),
                   jax.ShapeDtypeStruct((n, LANES), F32), jax.ShapeDtypeStruct((n // OUT_TM, 8, LANES), F32)],
        compiler_params=_cparams(("arbitrary",)),
        name="out_projection",
    )(merged, w_out16, x2d, mod_l, ln_g, ln_b, rwh, rwl, rb)


MOE_TM = 1024
MOE_RB = 256
MOE_CB = 512


def _moe_kernel(offs_ref, h_ref, comb_ref, wg_ref, wu_ref, wd_ref, o_ref, xs_scr, cs_scr, pm_scr, wg16, wu16, wd16):
    i = pl.program_id(0)
    e = pl.program_id(1)

    @pl.when(e == 0)
    def _():
        comb = comb_ref[...]
        lane_f = lax.broadcasted_iota(jnp.int32, comb.shape, 1).astype(F32)
        onehot = jnp.where(lane_f == comb[:, LANES - 1:LANES], 1.0, 0.0)
        r_t = lax.broadcasted_iota(jnp.int32, (MOE_TM, MOE_TM), 0)
        c_t = lax.broadcasted_iota(jnp.int32, (MOE_TM, MOE_TM), 1)
        tri = (c_t <= r_t).astype(BF16)
        rank = jnp.dot(tri, onehot.astype(BF16), preferred_element_type=F32)
        cnt = rank[MOE_TM - 1:MOE_TM, :]
        r_g = lax.broadcasted_iota(jnp.int32, (LANES, LANES), 0)
        c_g = lax.broadcasted_iota(jnp.int32, (LANES, LANES), 1)
        start = _dot_x2(cnt, (r_g < c_g).astype(BF16))
        pos = jnp.sum(onehot * (start + rank - 1.0), axis=-1, keepdims=True)
        pm = jnp.where(c_t.astype(F32) == pos, 1.0, 0.0).astype(BF16)
        pm_scr[...] = pm
        for cb in range(D_MODEL // MOE_CB):
            cols = slice(cb * MOE_CB, (cb + 1) * MOE_CB)
            xs_scr[:, cols] = lax.dot_general(pm, h_ref[:, cols], (((0,), (0,)), ((), ())),
                                              preferred_element_type=F32).astype(BF16)
        c_hi = comb.astype(BF16)
        c_mid = (comb - c_hi.astype(F32)).astype(BF16)
        c_lo = (comb - c_hi.astype(F32) - c_mid.astype(F32)).astype(BF16)
        tn = lambda y: lax.dot_general(pm, y, (((0,), (0,)), ((), ())), preferred_element_type=F32)
        cs_scr[...] = tn(c_hi) + tn(c_mid) + tn(c_lo)
        o_ref[...] = jnp.zeros_like(o_ref)

    g = e // MOE_PER_GROUP
    lo = offs_ref[i, g]
    hi = offs_ref[i, g + 1]

    @pl.when(hi > lo)
    def _():
        wg16[...] = wg_ref[...].astype(BF16)
        wu16[...] = wu_ref[...].astype(BF16)
        wd16[...] = wd_ref[...].astype(BF16)

        def block(j, carry):
            rows = pl.ds(pl.multiple_of(j * MOE_RB, MOE_RB), MOE_RB)
            x = xs_scr[rows, :]
            cs = cs_scr[rows, :]
            lane = lax.broadcasted_iota(jnp.int32, cs.shape, 1)
            ce = jnp.sum(jnp.where(lane == e, cs, 0.0), axis=-1, keepdims=True)
            hg = jnp.dot(x, wg16[...], preferred_element_type=F32)
            hu = jnp.dot(x, wu16[...], preferred_element_type=F32)
            act = _silu(hg) * hu * ce
            o_ref[rows, :] += jnp.dot(act.astype(BF16), wd16[...], preferred_element_type=F32)
            return carry

        lax.fori_loop(lo // MOE_RB, (hi + MOE_RB - 1) // MOE_RB, block, 0)

    @pl.when(e == pl.num_programs(1) - 1)
    def _():
        pm = pm_scr[...]
        for cb in range(D_MODEL // MOE_CB):
            cols = slice(cb * MOE_CB, (cb + 1) * MOE_CB)
            o_ref[:, cols] = jnp.dot(pm, o_ref[:, cols].astype(BF16), preferred_element_type=F32)


def _moe(h2, comb, offs, w_gate, w_up, w_down, layer):
    n = h2.shape[0]
    grid_spec = pltpu.PrefetchScalarGridSpec(
        num_scalar_prefetch=1,
        grid=(n // MOE_TM, MOE_EXPERTS),
        in_specs=[pl.BlockSpec((MOE_TM, D_MODEL), lambda i, e, offs: (i, 0)),
                  pl.BlockSpec((MOE_TM, LANES), lambda i, e, offs: (i, 0)),
                  pl.BlockSpec((None, None, D_MODEL, MOE_HID), lambda i, e, offs: (layer, e, 0, 0)),
                  pl.BlockSpec((None, None, D_MODEL, MOE_HID), lambda i, e, offs: (layer, e, 0, 0)),
                  pl.BlockSpec((None, None, MOE_HID, D_MODEL), lambda i, e, offs: (layer, e, 0, 0))],
        out_specs=pl.BlockSpec((MOE_TM, D_MODEL), lambda i, e, offs: (i, 0)),
        scratch_shapes=[pltpu.VMEM((MOE_TM, D_MODEL), BF16), pltpu.VMEM((MOE_TM, LANES), F32),
                        pltpu.VMEM((MOE_TM, MOE_TM), BF16), pltpu.VMEM((D_MODEL, MOE_HID), BF16),
                        pltpu.VMEM((D_MODEL, MOE_HID), BF16), pltpu.VMEM((MOE_HID, D_MODEL), BF16)])
    return pl.pallas_call(
        _moe_kernel,
        grid_spec=grid_spec,
        out_shape=jax.ShapeDtypeStruct((n, D_MODEL), F32),
        compiler_params=_cparams(("arbitrary", "arbitrary")),
        name="moe_experts",
    )(offs, h2, comb, w_gate, w_up, w_down)


def _group_offsets(cnt):
    per = MOE_TM // OUT_TM
    c = cnt[:, 0, :MOE_GROUPS].astype(jnp.int32).reshape(-1, per, MOE_GROUPS).sum(axis=1)
    return jnp.concatenate([jnp.zeros((c.shape[0], 1), jnp.int32), jnp.cumsum(c, axis=1)], axis=1)


FIN_TM = 512


def _final_kernel(x_ref, f_ref, mod_ref, lng_ref, lnb_ref, o_ref, *, req_base, rows_per_req):
    i = pl.program_id(0)
    m = req_base + (i * FIN_TM) // rows_per_req
    g2 = mod_ref[5, pl.ds(m, 1), :]
    o_ref[...] = _ln(ALPHA * x_ref[...] + g2 * f_ref[...]) * lng_ref[1:2, :] + lnb_ref[1:2, :]


def _final(x1, ffn, mod_l, ln_g, ln_b, layer, req_base, rows_per_req):
    n = x1.shape[0]
    row = pl.BlockSpec((FIN_TM, D_MODEL), lambda i: (i, 0))
    lnspec = pl.BlockSpec((None, 2, D_MODEL), lambda i: (layer, 0, 0))
    return pl.pallas_call(
        functools.partial(_final_kernel, req_base=req_base, rows_per_req=rows_per_req),
        grid=(n // FIN_TM,),
        in_specs=[row, row, pl.BlockSpec(mod_l.shape, lambda i: (0, 0, 0)), lnspec, lnspec],
        out_specs=row,
        out_shape=jax.ShapeDtypeStruct((n, D_MODEL), F32),
        compiler_params=_cparams(("arbitrary",)),
        name="final_norm",
    )(x1, ffn, mod_l, ln_g, ln_b)


def _block_diag_ones(width, blk):
    i = jnp.arange(width) // blk
    return (i[:, None] == i[None, :]).astype(BF16)


def _layer(x2d, layer, batch, seq, req_base, rows_per_req, mod_l, wts, consts, ctx):
    decode = ctx is not None
    p, h = _inproj(x2d, mod_l, wts['w_in'], layer, req_base, rows_per_req)

    ret = _retention(p, consts['ret_dec'][layer], ctx['ret'] if decode else None, batch, seq,
                     emit_state=not decode)
    cache = (ctx['k'], ctx['v'], consts['rope_cos'], consts['rope_sin']) if decode else None
    att = _attention(p, consts['gq'][layer], consts['gk'][layer], consts['bd_q'], consts['bd_k'],
                     batch, seq, cache)
    hc, hs = consts['hy_filt'][seq][layer]
    o_c = _hyena(p, wts['hy_conv'], wts['hy_bias'], hc, hs, consts['dft'][seq], layer, batch, seq)
    o_d, rw_state = _rwkv(p, consts['rw_mu'][layer], consts['rw_aup'][layer], wts['rw_g_up'][layer],
                          consts['rw_wup'][layer], wts['rw_w0'][layer], consts['rw_a0'][layer],
                          consts['rw_kk'][layer], consts['rw_ka'][layer], consts['rw_rk'][layer],
                          consts['bd_pair'], ctx['rw'] if decode else None, batch, seq)

    merged = _merge(h, (ret[0], att[0], o_c, o_d), wts['w_in'], wts['w_branch'], layer)
    x1, h2, comb, cnt = _outproj(merged, consts['w_out16'], x2d, mod_l, wts['ln_g'], wts['ln_b'],
                            consts['router_hi'][layer], consts['router_lo'][layer], consts['router_b'][layer],
                            layer, req_base, rows_per_req)
    ffn = _moe(h2, comb, _group_offsets(cnt), wts['moe_w_gate'], wts['moe_w_up'], wts['moe_w_down'], layer)
    x2 = _final(x1, ffn, mod_l, wts['ln_g'], wts['ln_b'], layer, req_base, rows_per_req)
    if decode:
        return x2, None
    return x2, (att[1], att[2], ret[1], rw_state)


def kernel(x_prompt, x_sample, c, cache_attn_k, cache_attn_v, state_ret, state_rwkv, c_ctx, mod_w, mod_b, w_in, ret_decay_exp, attn_q_norm, attn_k_norm, hy_conv, hy_w1, hy_b1, hy_w2, hy_b2, hy_freq, hy_w3, hy_log_rate, hy_bias, rw_mu, rw_w0, rw_w_up, rw_a0, rw_a_up, rw_g_up, rw_k_k, rw_k_a, rw_r_k, w_branch, w_out, ln_g, ln_b, moe_rg_w, moe_rg_b, moe_re_w, moe_re_b, moe_w_gate, moe_w_up, moe_w_down):
    batch, seq, _ = x_prompt.shape
    dbatch, dseq, _ = x_sample.shape
    past = cache_attn_k.shape[2]
    kvw = ATT_KV_HEADS * ATT_HD

    cvec = jnp.concatenate([c_ctx[None, :], c, jnp.zeros((8 - 1 - dbatch, D_MODEL), F32)], axis=0)
    mod = _modulation(cvec, mod_w, mod_b)

    wts = dict(w_in=w_in, hy_conv=hy_conv, hy_bias=hy_bias, rw_g_up=rw_g_up, rw_w0=rw_w0, w_branch=w_branch,
               ln_g=ln_g, ln_b=ln_b, moe_w_gate=moe_w_gate, moe_w_up=moe_w_up,
               moe_w_down=moe_w_down)

    zlo = jnp.zeros((DEPTH, RW_N, RW_C), F32)
    router_pad = LANES - MOE_EXPERTS - MOE_GROUPS
    router_w = jnp.pad(jnp.concatenate([moe_re_w, moe_rg_w], axis=2), ((0, 0), (0, 0), (0, router_pad)))
    router_hi = router_w.astype(BF16)
    router_lo = (router_w - router_hi.astype(F32)).astype(BF16)
    rope_cos, rope_sin = _rope_tables(dseq)
    consts = dict(
        ret_dec=jnp.repeat(ret_decay_exp, RET_DV, axis=-1),
        gq=jnp.tile(attn_q_norm, (1, ATT_HEADS))[:, None, :],
        gk=jnp.tile(attn_k_norm, (1, ATT_KV_HEADS))[:, None, :],
        bd_q=_block_diag_ones(ATT_HEADS * ATT_HD, ATT_HD),
        bd_k=_block_diag_ones(kvw, ATT_HD),
        bd_pair=_block_diag_ones(LANES, RW_N),
        rope_cos=rope_cos, rope_sin=rope_sin,
        rw_mu=rw_mu[:, None, :],
        rw_aup=jnp.concatenate([zlo, rw_a_up], axis=1),
        rw_wup=jnp.concatenate([rw_w_up, jnp.zeros_like(rw_w_up)], axis=2),
        rw_a0=rw_a0[:, None, :], rw_kk=rw_k_k[:, None, :], rw_ka=rw_k_a[:, None, :],
        rw_rk=rw_r_k.reshape(DEPTH, 1, RW_C),
        w_out16=w_out.astype(BF16),
        router_hi=router_hi, router_lo=router_lo,
        router_b=jnp.pad(jnp.concatenate([moe_re_b, moe_rg_b], axis=1), ((0, 0), (0, router_pad)))[:, None, :],
        dft={}, hy_filt={},
    )
    w1p = jnp.pad(hy_w1, ((0, 0), (0, LANES - HY_FEAT), (0, 0)))
    for s in sorted({seq, dseq}):
        dft = _dft_matrices(s)
        z = _hyena_feats(s)
        consts['dft'][s] = dft
        consts['hy_filt'][s] = [
            _hyena_filters(z, w1p[l], hy_b1[l][None, :], hy_w2[l], hy_b2[l][None, :], hy_freq[l], hy_w3[l],
                           hy_log_rate[l].reshape(1, 2 * HY_C), dft, s)
            for l in range(DEPTH)]

    y = x_prompt.reshape(batch * seq, D_MODEL)
    ks, vs, rets, rws = [], [], [], []
    for l in range(DEPTH):
        y, (k_l, v_l, ret_l, rw_l) = _layer(y, l, batch, seq, 0, batch * seq, mod[l], wts, consts, None)
        ks.append(k_l)
        vs.append(v_l)
        rets.append(ret_l)
        rws.append(_unpair_states(rw_l))
    y_prompt = y.reshape(batch, seq, D_MODEL)
    new_k = jnp.stack(ks, 1).reshape(batch, DEPTH, seq, ATT_KV_HEADS, ATT_HD)
    new_v = jnp.stack(vs, 1).reshape(batch, DEPTH, seq, ATT_KV_HEADS, ATT_HD)
    new_ret = jnp.stack(rets, 1)
    new_rw = jnp.stack(rws, 1)

    ys = x_sample.reshape(dbatch * dseq, D_MODEL)
    for l in range(DEPTH):
        ctx = dict(k=cache_attn_k[:, l].reshape(dbatch, past, kvw), v=cache_attn_v[:, l].reshape(dbatch, past, kvw),
                   ret=state_ret[:, l], rw=_pair_states(state_rwkv[:, l]))
        ys, _ = _layer(ys, l, dbatch, dseq, 1, dseq, mod[l], wts, consts, ctx)
    y_sample = ys.reshape(dbatch, dseq, D_MODEL)

    return (y_prompt, y_sample, new_k, new_v, new_ret, new_rw)
```
